```python
import jax, jax.numpy as jnp
from jax import lax
import numpy as np

D_MODEL = 1024
BATCH = 4
SEQ = 8192
DEPTH = 2
DEC_BATCH = 32
DEC_SEQ = 8
PAST_LEN = 16384
PAGE_SIZE = 128

N_AB_LAYERS = (DEPTH + 1) // 2
N_C_LAYERS = DEPTH // 2
ATT_HEADS = 8
HEAD_DIM = 64
ATT_WIDTH = ATT_HEADS * HEAD_DIM
MOBA_BLOCK = 256
MOBA_TOPK = 3
QUERY_BLOCK = 128
LRU_WIDTH = 512
LRU_HEADS = 8
LRU_HEAD_DIM = LRU_WIDTH // LRU_HEADS
CONV_WIDTH = 4
LRU_C = 8.0
AB_IN = 4 * ATT_WIDTH + 2 * LRU_WIDTH
AB_MIX = ATT_WIDTH + LRU_WIDTH
AB_SPLITS = (ATT_WIDTH, 2 * ATT_WIDTH, 3 * ATT_WIDTH, 4 * ATT_WIDTH, 4 * ATT_WIDTH + LRU_WIDTH)
GMLP_WIDTH = 1024
GMLP_GROUPS = 8
GMLP_GROUP_DIM = GMLP_WIDTH // GMLP_GROUPS
CHUNK = 128
NORM_EPS = 1e-6

kernel_name = 'moba_rglru_gmlp_hybrid_step'


def _rms_norm(x, g):
    xf = x.astype(jnp.float32)
    y = xf * lax.rsqrt(jnp.mean(xf * xf, axis=-1, keepdims=True) + NORM_EPS)
    return (y * g.astype(jnp.float32)).astype(x.dtype)


def _layer_norm(x, g, b):
    xf = x.astype(jnp.float32)
    mu = jnp.mean(xf, axis=-1, keepdims=True)
    var = jnp.mean(jnp.square(xf - mu), axis=-1, keepdims=True)
    y = (xf - mu) * lax.rsqrt(var + NORM_EPS)
    return (y * g.astype(jnp.float32) + b.astype(jnp.float32)).astype(x.dtype)


def _select_blocks(q, k_mean, n_full):
    nb = k_mean.shape[1]
    k_eff = min(MOBA_TOPK, nb)
    gate = jnp.einsum('bqhd,bnhd->bqhn', q.astype(jnp.float32), k_mean)
    past = jnp.arange(nb)[None, :] < n_full[:, None]
    gate = jnp.where(past[None, :, None, :], gate, -jnp.inf)
    _, idx = lax.top_k(gate, k_eff)
    ok = jnp.arange(k_eff)[None, :] < n_full[:, None]
    return idx, ok[None, :, None, :]


def _attend(q, k_sel, v_sel, sel_ok, k_own, v_own, own_ok):
    scale = HEAD_DIM ** -0.5
    s_own = jnp.einsum('bqhd,brhd->bqhr', q, k_own).astype(jnp.float32) * scale
    s_own = jnp.where(own_ok[None, :, None, :], s_own, -jnp.inf)
    if k_sel is None:
        p_own = jax.nn.softmax(s_own, axis=-1)
        return jnp.einsum('bqhr,brhd->bqhd', p_own.astype(v_own.dtype), v_own)
    b, nq, h, n_sel, blk, _ = k_sel.shape
    s_sel = jnp.einsum('bqhd,bqhjmd->bqhjm', q, k_sel).astype(jnp.float32) * scale
    s_sel = jnp.where(sel_ok[..., None], s_sel, -jnp.inf).reshape(b, nq, h, n_sel * blk)
    p = jax.nn.softmax(jnp.concatenate([s_sel, s_own], axis=-1), axis=-1)
    p_sel = p[..., : n_sel * blk].reshape(b, nq, h, n_sel, blk).astype(v_sel.dtype)
    p_own = p[..., n_sel * blk:].astype(v_own.dtype)
    return (jnp.einsum('bqhjm,bqhjmd->bqhd', p_sel, v_sel)
            + jnp.einsum('bqhr,brhd->bqhd', p_own, v_own))


def _moba_prompt(q, k, v):
    b, s, h, dh = q.shape
    nb = -(-s // MOBA_BLOCK)
    pad = nb * MOBA_BLOCK - s
    kb = jnp.pad(k, ((0, 0), (0, pad), (0, 0), (0, 0))).reshape(b, nb, MOBA_BLOCK, h, dh)
    vb = jnp.pad(v, ((0, 0), (0, pad), (0, 0), (0, 0))).reshape(b, nb, MOBA_BLOCK, h, dh)
    k_mean = kb.astype(jnp.float32).mean(axis=2)
    b_idx = jnp.arange(b)[:, None, None, None, None]
    h_idx = jnp.arange(h)[None, None, :, None, None]
    r_idx = jnp.arange(MOBA_BLOCK)

    def one_query_block(i):
        start = i * QUERY_BLOCK
        qb = lax.dynamic_slice_in_dim(q, start, QUERY_BLOCK, axis=1)
        pos = start + jnp.arange(QUERY_BLOCK)
        idx, sel_ok = _select_blocks(qb, k_mean, pos // MOBA_BLOCK)
        k_sel = kb[b_idx, idx[..., None], r_idx, h_idx]
        v_sel = vb[b_idx, idx[..., None], r_idx, h_idx]
        c0 = start // MOBA_BLOCK
        k_own = lax.dynamic_index_in_dim(kb, c0, axis=1, keepdims=False)
        v_own = lax.dynamic_index_in_dim(vb, c0, axis=1, keepdims=False)
        own_ok = (c0 * MOBA_BLOCK + r_idx)[None, :] <= pos[:, None]
        return _attend(qb, k_sel, v_sel, sel_ok, k_own, v_own, own_ok)

    out = lax.map(one_query_block, jnp.arange(s // QUERY_BLOCK))
    return out.transpose(1, 0, 2, 3, 4).reshape(b, s, ATT_WIDTH)


def _moba_sample(q, k_new, v_new, cache_k, cache_v, page_table, layer):
    b, t, h, _ = q.shape
    n_pages = PAST_LEN // PAGE_SIZE
    bp = MOBA_BLOCK // PAGE_SIZE
    nb_past = PAST_LEN // MOBA_BLOCK
    c0 = PAST_LEN // MOBA_BLOCK
    r_cached = PAST_LEN - c0 * MOBA_BLOCK
    pos = PAST_LEN + jnp.arange(t)
    k_sel = v_sel = sel_ok = None
    if nb_past > 0:
        full_pages = page_table[:, : nb_past * bp]
        page_sum = cache_k[layer, full_pages].astype(jnp.float32).sum(axis=2)
        k_mean = page_sum.reshape(b, nb_past, bp, h, HEAD_DIM).sum(axis=2) / MOBA_BLOCK
        idx, sel_ok = _select_blocks(q, k_mean, pos // MOBA_BLOCK)
        n_sel = idx.shape[-1]
        logical = idx[..., None] * bp + jnp.arange(bp)
        phys = page_table[jnp.arange(b)[:, None, None, None, None], logical]
        rows = (layer, phys[..., None], jnp.arange(PAGE_SIZE),
                jnp.arange(h)[None, None, :, None, None, None])
        k_sel = cache_k[rows].reshape(b, t, h, n_sel, MOBA_BLOCK, HEAD_DIM)
        v_sel = cache_v[rows].reshape(b, t, h, n_sel, MOBA_BLOCK, HEAD_DIM)
    if r_cached > 0:
        own_pages = page_table[:, c0 * bp: n_pages]
        k_old = cache_k[layer, own_pages].reshape(b, r_cached, h, HEAD_DIM).astype(k_new.dtype)
        v_old = cache_v[layer, own_pages].reshape(b, r_cached, h, HEAD_DIM).astype(v_new.dtype)
        k_own = jnp.concatenate([k_old, k_new], axis=1)
        v_own = jnp.concatenate([v_old, v_new], axis=1)
    else:
        k_own, v_own = k_new, v_new
    key_pos = c0 * MOBA_BLOCK + jnp.arange(r_cached + t)
    own_ok = key_pos[None, :] <= pos[:, None]
    return _attend(q, k_sel, v_sel, sel_ok, k_own, v_own, own_ok).reshape(b, t, ATT_WIDTH)


def _rg_lru(x, h0, conv_state, conv_w, conv_b, wa, ba, wx, bx, lam):
    b, t, w = x.shape
    xin = jnp.concatenate([conv_state.astype(x.dtype), x], axis=1)
    xc = lax.conv_general_dilated(xin, conv_w[:, None, :].astype(x.dtype), window_strides=(1,),
                                  padding='VALID', dimension_numbers=('NWC', 'WIO', 'NWC'),
                                  feature_group_count=w) + conv_b
    xg = xc.reshape(b, t, LRU_HEADS, LRU_HEAD_DIM)
    r = jax.nn.sigmoid(jnp.einsum('bthi,hij->bthj', xg, wa).reshape(b, t, w) + ba).astype(jnp.float32)
    i = jax.nn.sigmoid(jnp.einsum('bthi,hij->bthj', xg, wx).reshape(b, t, w) + bx).astype(jnp.float32)
    log_a = -LRU_C * r * jax.nn.softplus(-lam.astype(jnp.float32))
    a = jnp.exp(log_a)
    u = jnp.sqrt(-jnp.expm1(2.0 * log_a)) * (i * xc.astype(jnp.float32))
    u = u.at[:, 0].add(a[:, 0] * h0.astype(jnp.float32))
    _, hs = lax.associative_scan(lambda l, rr: (l[0] * rr[0], rr[0] * l[1] + rr[1]), (a, u), axis=1)
    return hs.astype(x.dtype), hs[:, -1], xin[:, -(CONV_WIDTH - 1):]


def _ab_mixer(xn, attn_fn, h0, conv_state, w_in, conv_w, conv_b, wa, ba, wx, bx, lam, w_out):
    b, t, _ = xn.shape
    q, k, v, g_att, x_lru, g_lru = jnp.split(xn @ w_in, AB_SPLITS, axis=-1)
    q = q.reshape(b, t, ATT_HEADS, HEAD_DIM)
    k = k.reshape(b, t, ATT_HEADS, HEAD_DIM)
    v = v.reshape(b, t, ATT_HEADS, HEAD_DIM)
    att = attn_fn(q, k, v)
    rec, h_last, conv_last = _rg_lru(x_lru, h0, conv_state, conv_w, conv_b, wa, ba, wx, bx, lam)
    mixed = jnp.concatenate([jax.nn.silu(g_att) * att, jax.nn.silu(g_lru) * rec], axis=-1)
    return mixed @ w_out, k, v, h_last, conv_last


def _chunk_spatial_gating(u, v, ws, bs):
    b, t, w = u.shape
    nck = -(-t // CHUNK)
    pad = nck * CHUNK - t
    vp = jnp.pad(v, ((0, 0), (0, pad), (0, 0))).reshape(b, nck, CHUNK, GMLP_GROUPS, GMLP_GROUP_DIM)
    causal = jnp.tril(jnp.ones((CHUNK, CHUNK), dtype=bool))
    wm = jnp.where(causal[None], ws, jnp.zeros_like(ws))
    mix = jnp.einsum('gts,bcsgd->bctgd', wm, vp) + bs.T[None, None, :, :, None]
    mix = mix.reshape(b, nck * CHUNK, w)[:, :t]
    return u * mix


def _c_mixer(xn, w_in, ln_g, ln_b, ws, bs, w_out):
    u, v, g = jnp.split(xn @ w_in, (GMLP_WIDTH, 2 * GMLP_WIDTH), axis=-1)
    u = jax.nn.gelu(u)
    v = _layer_norm(jax.nn.gelu(v), ln_g, ln_b)
    s = _chunk_spatial_gating(u, v, ws, bs)
    return (jax.nn.silu(g) * s) @ w_out, v


def setup_inputs(seed: int = 0) -> dict:
    key = jax.random.key(seed)
    ks = jax.random.split(key, 26)
    f32 = jnp.float32
    n_pages = PAST_LEN // PAGE_SIZE
    n_used = DEC_BATCH * n_pages
    n_phys = n_used + max(1, n_used // 4)
    nrm = lambda k, shape, s: s * jax.random.normal(k, shape, f32)
    page_table = jax.random.permutation(ks[0], n_phys)[:n_used].reshape(DEC_BATCH, n_pages).astype(jnp.int32)
    lam_u = jax.random.uniform(ks[1], (N_AB_LAYERS, LRU_WIDTH), f32, minval=0.9, maxval=0.999)
    a_base = lam_u ** (1.0 / LRU_C)
    lru_lambda = jnp.log(a_base) - jnp.log1p(-a_base)
    return {
        'x_prompt': nrm(ks[2], (BATCH, SEQ, D_MODEL), 1.0),
        'x_sample': nrm(ks[3], (DEC_BATCH, DEC_SEQ, D_MODEL), 1.0),
        'cache_k': nrm(ks[4], (N_AB_LAYERS, n_phys, PAGE_SIZE, ATT_HEADS, HEAD_DIM), 1.0),
        'cache_v': nrm(ks[5], (N_AB_LAYERS, n_phys, PAGE_SIZE, ATT_HEADS, HEAD_DIM), 1.0),
        'page_table': page_table,
        'state_lru_h': nrm(ks[6], (N_AB_LAYERS, DEC_BATCH, LRU_WIDTH), 0.5),
        'state_conv': nrm(ks[7], (N_AB_LAYERS, DEC_BATCH, CONV_WIDTH - 1, LRU_WIDTH), 1.0),
        'norm_pre': 1.0 + nrm(ks[8], (DEPTH, D_MODEL), 0.05),
        'norm_post': 1.0 + nrm(ks[9], (DEPTH, D_MODEL), 0.05),
        'w_in_ab': nrm(ks[10], (N_AB_LAYERS, D_MODEL, AB_IN), D_MODEL ** -0.5),
        'conv_w': nrm(ks[11], (N_AB_LAYERS, CONV_WIDTH, LRU_WIDTH), CONV_WIDTH ** -0.5),
        'conv_b': nrm(ks[12], (N_AB_LAYERS, LRU_WIDTH), 0.02),
        'lru_wa': nrm(ks[13], (N_AB_LAYERS, LRU_HEADS, LRU_HEAD_DIM, LRU_HEAD_DIM), LRU_HEAD_DIM ** -0.5),
        'lru_ba': nrm(ks[14], (N_AB_LAYERS, LRU_WIDTH), 0.02),
        'lru_wx': nrm(ks[15], (N_AB_LAYERS, LRU_HEADS, LRU_HEAD_DIM, LRU_HEAD_DIM), LRU_HEAD_DIM ** -0.5),
        'lru_bx': nrm(ks[16], (N_AB_LAYERS, LRU_WIDTH), 0.02),
        'lru_lambda': lru_lambda,
        'w_out_ab': nrm(ks[17], (N_AB_LAYERS, AB_MIX, D_MODEL), AB_MIX ** -0.5),
        'w_in_c': nrm(ks[18], (N_C_LAYERS, D_MODEL, 3 * GMLP_WIDTH), D_MODEL ** -0.5),
        'c_ln_g': 1.0 + nrm(ks[19], (N_C_LAYERS, GMLP_WIDTH), 0.05),
        'c_ln_b': nrm(ks[20], (N_C_LAYERS, GMLP_WIDTH), 0.02),
        'c_ws': nrm(ks[21], (N_C_LAYERS, GMLP_GROUPS, CHUNK, CHUNK), CHUNK ** -0.5),
        'c_bs': 1.0 + nrm(ks[22], (N_C_LAYERS, GMLP_GROUPS, CHUNK), 0.05),
        'w_out_c': nrm(ks[23], (N_C_LAYERS, GMLP_WIDTH, D_MODEL), GMLP_WIDTH ** -0.5),
    }


def reference(x_prompt, x_sample, cache_k, cache_v, page_table, state_lru_h, state_conv,
              norm_pre, norm_post, w_in_ab, conv_w, conv_b, lru_wa, lru_ba, lru_wx, lru_bx,
              lru_lambda, w_out_ab, w_in_c, c_ln_g, c_ln_b, c_ws, c_bs, w_out_c):
    yp, ys = x_prompt, x_sample
    bp_, bs_ = x_prompt.shape[0], x_sample.shape[0]
    kp_l, vp_l, hp_l, cp_l = [], [], [], []
    ks_l, vs_l, hs_l, cs_l, gv_l = [], [], [], [], []
    for layer in range(DEPTH):
        j = layer // 2
        xpn = _rms_norm(yp, norm_pre[layer])
        xsn = _rms_norm(ys, norm_pre[layer])
        if layer % 2 == 0:
            prm = (w_in_ab[j], conv_w[j], conv_b[j], lru_wa[j], lru_ba[j], lru_wx[j], lru_bx[j],
                   lru_lambda[j], w_out_ab[j])
            h0 = jnp.zeros((bp_, LRU_WIDTH), jnp.float32)
            c_init = jnp.zeros((bp_, CONV_WIDTH - 1, LRU_WIDTH), xpn.dtype)
            op, kp, vp, hp, cp = _ab_mixer(xpn, _moba_prompt, h0, c_init, *prm)
            attn_s = lambda q, k, v, jj=j: _moba_sample(q, k, v, cache_k, cache_v, page_table, jj)
            os_, kss, vss, hss, css = _ab_mixer(xsn, attn_s, state_lru_h[j], state_conv[j], *prm)
            kp_l.append(kp); vp_l.append(vp); hp_l.append(hp); cp_l.append(cp)
            ks_l.append(kss); vs_l.append(vss); hs_l.append(hss); cs_l.append(css)
        else:
            prm = (w_in_c[j], c_ln_g[j], c_ln_b[j], c_ws[j], c_bs[j], w_out_c[j])
            op, _ = _c_mixer(xpn, *prm)
            os_, gv = _c_mixer(xsn, *prm)
            gv_l.append(gv)
        yp = yp + _rms_norm(op, norm_post[layer])
        ys = ys + _rms_norm(os_, norm_post[layer])
    return (yp, ys, jnp.stack(kp_l), jnp.stack(vp_l), jnp.stack(hp_l), jnp.stack(cp_l),
            jnp.stack(ks_l), jnp.stack(vs_l), jnp.stack(hs_l), jnp.stack(cs_l), jnp.stack(gv_l))
```

```python
import functools

import jax
import jax.numpy as jnp
from jax import lax
from jax.experimental import pallas as pl
from jax.experimental.pallas import tpu as pltpu

ATT_HEADS = 8
HEAD_DIM = 64
ATT_WIDTH = ATT_HEADS * HEAD_DIM
HEAD_PAIRS = ATT_HEADS // 2
MOBA_BLOCK = 256
MOBA_TOPK = 3
QUERY_BLOCK = 128
LRU_WIDTH = 512
CONV_WIDTH = 4
LRU_C = 8.0
GMLP_WIDTH = 1024
GMLP_GROUPS = 8
GMLP_GROUP_DIM = GMLP_WIDTH // GMLP_GROUPS
CHUNK = 128
PAGE_SIZE = 128
NORM_EPS = 1e-6
ATT_SCALE = HEAD_DIM ** -0.5

V7X_LANES = 128
V7X_SUBLANES = 8
VMEM_LIMIT_BYTES = 56 * 1024 * 1024

F32 = jnp.float32
BF16 = jnp.bfloat16
NEG_INF = float("-inf")
POS_INF = float("inf")


def _rms_norm(x, g):
    return x * lax.rsqrt(jnp.mean(x * x, axis=-1, keepdims=True) + NORM_EPS) * g


def _layer_norm(x, g, b):
    mu = jnp.mean(x, axis=-1, keepdims=True)
    xc = x - mu
    var = jnp.mean(xc * xc, axis=-1, keepdims=True)
    return xc * lax.rsqrt(var + NORM_EPS) * g + b


def _sigmoid(x):
    return 1.0 / (1.0 + jnp.exp(-x))


def _silu(x):
    return x * _sigmoid(x)


def _gelu_tanh(x):
    c = 0.7978845608028654
    return 0.5 * x * (1.0 + jnp.tanh(c * (x + 0.044715 * (x * x * x))))


def _softplus(x):
    return jnp.maximum(x, 0.0) + jnp.log1p(jnp.exp(-jnp.abs(x)))


def _dot(a, b):
    return jnp.dot(a, b, preferred_element_type=F32)


def _dot_exact(a, b):
    return jnp.dot(a, b, preferred_element_type=F32, precision=lax.Precision.HIGHEST)


def _split_bf16(x):
    hi = x.astype(BF16)
    lo = (x - hi.astype(F32)).astype(BF16)
    return hi, lo


def _in_proj_ab(xn, whi_ref, wlo_ref):
    qk_w = 2 * ATT_WIDTH
    xh, xl = _split_bf16(xn)
    qk = (_dot(xh, whi_ref[:, :qk_w]) + _dot(xh, wlo_ref[...])
          + _dot(xl, whi_ref[:, :qk_w]))
    rest = _dot(xh, whi_ref[:, qk_w:])
    return qk, rest


def _lru_coeffs(xc, wa_ref, wx_ref, ba_ref, bx_ref, lam_ref):
    xcb = xc.astype(BF16)
    r = _sigmoid(_dot(xcb, wa_ref[...]) + ba_ref[...])
    i = _sigmoid(_dot(xcb, wx_ref[...]) + bx_ref[...])
    log_a = (-LRU_C) * r * _softplus(-lam_ref[...])
    a = jnp.exp(log_a)
    u = jnp.sqrt(-jnp.tanh(log_a) * (a * a + 1.0)) * (i * xc)
    return a, u


def _top3_rows(gate, blk, n_valid):
    nb = gate.shape[0]
    sel = jnp.zeros(gate.shape, F32)
    for r in range(MOBA_TOPK):
        mx = jnp.max(gate, axis=0, keepdims=True)
        ix = jnp.min(jnp.where(gate == mx, blk, nb), axis=0, keepdims=True)
        hit = blk == ix
        sel = jnp.maximum(sel, jnp.where(hit, (r < n_valid).astype(F32), 0.0))
        gate = jnp.where(hit, NEG_INF, gate)
    return sel


def _prompt_in_kernel(x_ref, g_ref, whi_ref, wlo_ref, cw_ref, cb_ref, wa_ref, wx_ref,
                      ba_ref, bx_ref, lam_ref,
                      qt_ref, kt_ref, vt_ref, vtb_ref, kn_ref, km_ref, sg_ref, ml_ref,
                      hl_ref, cl_ref,
                      xbuf, hcar, abuf, ubuf):
    t = pl.program_id(1)
    tm = x_ref.shape[0]
    w = LRU_WIDTH

    @pl.when(t == 0)
    def _():
        xbuf[0:V7X_SUBLANES, :] = jnp.zeros((V7X_SUBLANES, w), F32)
        hcar[...] = jnp.zeros(hcar.shape, F32)

    xn = _rms_norm(x_ref[...], g_ref[...])
    qk, rest = _in_proj_ab(xn, whi_ref, wlo_ref)
    q = qk[:, :ATT_WIDTH]
    k = qk[:, ATT_WIDTH:]
    v = rest[:, :ATT_WIDTH]
    g_att = rest[:, ATT_WIDTH:2 * ATT_WIDTH]
    x_lru = rest[:, 2 * ATT_WIDTH:2 * ATT_WIDTH + w]
    g_lru = rest[:, 2 * ATT_WIDTH + w:]

    for s in range(tm // QUERY_BLOCK):
        rows = slice(s * QUERY_BLOCK, (s + 1) * QUERY_BLOCK)
        qt_ref[s] = q[rows, :].T
        vtb_ref[s] = v[rows, :].T.astype(BF16)
    kt_ref[...] = k.T
    vt_ref[...] = v.T
    kn_ref[...] = k.astype(BF16)
    for s in range(tm // MOBA_BLOCK):
        km_ref[pl.ds(t * (tm // MOBA_BLOCK) + s, 1), :] = jnp.mean(
            k[s * MOBA_BLOCK:(s + 1) * MOBA_BLOCK, :], axis=0, keepdims=True)
    sg_ref[...] = _silu(g_att).astype(BF16)

    xbuf[V7X_SUBLANES:, :] = x_lru
    xc = cw_ref[CONV_WIDTH - 1:CONV_WIDTH, :] * x_lru + cb_ref[...]
    for back in range(1, CONV_WIDTH):
        xc = xc + (cw_ref[CONV_WIDTH - 1 - back:CONV_WIDTH - back, :]
                   * xbuf[pl.ds(V7X_SUBLANES - back, tm), :])
    xbuf[0:V7X_SUBLANES, :] = x_lru[tm - V7X_SUBLANES:, :]

    a, u = _lru_coeffs(xc, wa_ref, wx_ref, ba_ref, bx_ref, lam_ref)
    abuf[...] = a
    ubuf[...] = u

    row = lax.broadcasted_iota(jnp.int32, (V7X_SUBLANES, w), 0)

    def group(gi, h):
        r0 = pl.multiple_of(gi * V7X_SUBLANES, V7X_SUBLANES)
        ag = abuf[pl.ds(r0, V7X_SUBLANES), :]
        ug = ubuf[pl.ds(r0, V7X_SUBLANES), :]
        for d in (1, 2, 4):
            keep = row >= d
            ug = jnp.where(keep, ag * pltpu.roll(ug, d, 0) + ug, ug)
            ag = jnp.where(keep, ag * pltpu.roll(ag, d, 0), ag)
        hg = ag * h + ug
        ubuf[pl.ds(r0, V7X_SUBLANES), :] = hg
        return jnp.broadcast_to(hg[V7X_SUBLANES - 1:, :], (V7X_SUBLANES, w))

    hcar[...] = lax.fori_loop(0, tm // V7X_SUBLANES, group, hcar[...])
    hs = ubuf[...]
    ml_ref[...] = (_silu(g_lru) * hs).astype(BF16)

    @pl.when(t == pl.num_programs(1) - 1)
    def _():
        hl_ref[...] = hs[tm - 1:, :]
        cl_ref[...] = x_lru[tm - (CONV_WIDTH - 1):, :]


def _prompt_in(x, g_pre, whi, wlo, cw, cb, wa, wx, ba, bx, lam, tm):
    b, s, d = x.shape
    w = LRU_WIDTH
    nb = s // MOBA_BLOCK
    nq = s // QUERY_BLOCK
    const = lambda shape: pl.BlockSpec(shape, lambda bi, ti: (0,) * len(shape))
    out_shape = (
        jax.ShapeDtypeStruct((b, nq, ATT_WIDTH, QUERY_BLOCK), F32),
        jax.ShapeDtypeStruct((b, ATT_WIDTH, s), F32),
        jax.ShapeDtypeStruct((b, ATT_WIDTH, s), F32),
        jax.ShapeDtypeStruct((b, nq, ATT_WIDTH, QUERY_BLOCK), BF16),
        jax.ShapeDtypeStruct((b, s, ATT_WIDTH), BF16),
        jax.ShapeDtypeStruct((b, nb, ATT_WIDTH), F32),
        jax.ShapeDtypeStruct((b, s, ATT_WIDTH), BF16),
        jax.ShapeDtypeStruct((b, s, w), BF16),
        jax.ShapeDtypeStruct((b, 1, w), F32),
        jax.ShapeDtypeStruct((b, CONV_WIDTH - 1, w), F32),
    )
    out_specs = (
        pl.BlockSpec((None, tm // QUERY_BLOCK, ATT_WIDTH, QUERY_BLOCK), lambda bi, ti: (bi, ti, 0, 0)),
        pl.BlockSpec((None, ATT_WIDTH, tm), lambda bi, ti: (bi, 0, ti)),
        pl.BlockSpec((None, ATT_WIDTH, tm), lambda bi, ti: (bi, 0, ti)),
        pl.BlockSpec((None, tm // QUERY_BLOCK, ATT_WIDTH, QUERY_BLOCK), lambda bi, ti: (bi, ti, 0, 0)),
        pl.BlockSpec((None, tm, ATT_WIDTH), lambda bi, ti: (bi, ti, 0)),
        pl.BlockSpec((None, nb, ATT_WIDTH), lambda bi, ti: (bi, 0, 0)),
        pl.BlockSpec((None, tm, ATT_WIDTH), lambda bi, ti: (bi, ti, 0)),
        pl.BlockSpec((None, tm, w), lambda bi, ti: (bi, ti, 0)),
        pl.BlockSpec((None, 1, w), lambda bi, ti: (bi, 0, 0)),
        pl.BlockSpec((None, CONV_WIDTH - 1, w), lambda bi, ti: (bi, 0, 0)),
    )
    in_specs = [
        pl.BlockSpec((None, tm, d), lambda bi, ti: (bi, ti, 0)),
        const((1, d)), const(whi.shape), const(wlo.shape), const(cw.shape), const(cb.shape),
        const(wa.shape), const(wx.shape), const(ba.shape), const(bx.shape), const(lam.shape),
    ]
    return pl.pallas_call(
        _prompt_in_kernel,
        grid=(b, s // tm),
        in_specs=in_specs,
        out_specs=out_specs,
        out_shape=out_shape,
        scratch_shapes=[
            pltpu.VMEM((tm + V7X_SUBLANES, w), F32),
            pltpu.VMEM((V7X_SUBLANES, w), F32),
            pltpu.VMEM((tm, w), F32),
            pltpu.VMEM((tm, w), F32),
        ],
        compiler_params=pltpu.CompilerParams(
            dimension_semantics=("arbitrary", "arbitrary"),
            vmem_limit_bytes=VMEM_LIMIT_BYTES),
        name="prompt_in",
    )(x, g_pre, whi, wlo, cw, cb, wa, wx, ba, bx, lam)


def _prompt_attn_kernel(qt_ref, kn_ref, vtb_ref, km_ref, o_ref, vaug, sel_ref, m_ref, acc_ref):
    nq = qt_ref.shape[0]
    nb = km_ref.shape[0]
    qb = QUERY_BLOCK
    half = HEAD_DIM
    pair_rows = lax.broadcasted_iota(jnp.int32, (2 * half, qb), 0)
    first = pair_rows < half
    one = jnp.ones((), BF16)

    def build(j, c):
        vt = vtb_ref[j]
        vaug[0, j] = jnp.where(first, vt, one)
        vaug[1, j] = jnp.where(first, one, vt)
        return c

    lax.fori_loop(0, nq, build, 0)

    kmp = km_ref[...]
    blk = lax.broadcasted_iota(jnp.int32, (nb, qb), 0)
    key_i = lax.broadcasted_iota(jnp.int32, (qb, qb), 0)
    qry_i = lax.broadcasted_iota(jnp.int32, (qb, qb), 1)
    causal = key_i <= qry_i

    def update(h, s, selrow, vparts):
        cm = jnp.max(s, axis=0, keepdims=True)
        m_old = m_ref[h]
        if selrow is not None:
            cm = jnp.where(selrow > 0.0, cm, NEG_INF)
        m_new = jnp.maximum(m_old, cm)
        shift = m_new if selrow is None else jnp.where(selrow > 0.0, m_new, POS_INF)
        p = jnp.exp(s - shift).astype(BF16)
        alpha = jnp.exp(m_old - m_new)
        pv = _dot(vparts[0], p[:qb, :])
        for i in range(1, len(vparts)):
            pv = pv + _dot(vparts[i], p[i * qb:(i + 1) * qb, :])
        acc_ref[h] = acc_ref[h] * alpha + pv
        m_ref[h] = m_new

    def query_block(qi, c):
        c0 = qi // 2
        qt = qt_ref[qi]
        qth = (jnp.where(first, qt, 0.0), jnp.where(first, 0.0, qt))
        qtb = tuple((x * ATT_SCALE).astype(BF16) for x in qth)
        for h in range(2):
            gate = _dot_exact(kmp, qth[h])
            gate = jnp.where(blk < c0, gate, NEG_INF)
            sel_ref[h] = _top3_rows(gate, blk, c0)

        kd = kn_ref[pl.ds(pl.multiple_of(qi * qb, qb), qb), :]
        for h in range(2):
            s = jnp.where(causal, _dot(kd, qtb[h]), NEG_INF)
            m = jnp.max(s, axis=0, keepdims=True)
            p = jnp.exp(s - m).astype(BF16)
            acc_ref[h] = _dot(vaug[h, qi], p)
            m_ref[h] = m

        @pl.when(qi % 2 == 1)
        def _():
            kl = kn_ref[pl.ds(pl.multiple_of((qi - 1) * qb, qb), qb), :]
            for h in range(2):
                update(h, _dot(kl, qtb[h]), None, (vaug[h, qi - 1],))

        def past_block(j, cc):
            kb = kn_ref[pl.ds(pl.multiple_of(j * MOBA_BLOCK, MOBA_BLOCK), MOBA_BLOCK), :]
            for h in range(2):
                selrow = sel_ref[h, pl.ds(j, 1), :]
                update(h, _dot(kb, qtb[h]), selrow, (vaug[h, 2 * j], vaug[h, 2 * j + 1]))
            return cc

        lax.fori_loop(0, c0, past_block, 0)

        a0 = acc_ref[0]
        a1 = acc_ref[1]
        ot = jnp.concatenate([a0[:half, :] / a0[half:half + 1, :],
                              a1[half:, :] / a1[0:1, :]], axis=0)
        o_ref[pl.ds(pl.multiple_of(qi * qb, qb), qb), :] = ot.T
        return c

    lax.fori_loop(0, nq, query_block, 0)


def _prompt_attn(qt, kn, vtb, km):
    b, nq, _, qb = qt.shape
    s = kn.shape[1]
    nb = km.shape[1]
    pw = 2 * HEAD_DIM
    return pl.pallas_call(
        _prompt_attn_kernel,
        grid=(b, HEAD_PAIRS),
        in_specs=[
            pl.BlockSpec((None, nq, pw, qb), lambda bi, pi: (bi, 0, pi, 0)),
            pl.BlockSpec((None, s, pw), lambda bi, pi: (bi, 0, pi)),
            pl.BlockSpec((None, nq, pw, qb), lambda bi, pi: (bi, 0, pi, 0)),
            pl.BlockSpec((None, nb, pw), lambda bi, pi: (bi, 0, pi)),
        ],
        out_specs=pl.BlockSpec((None, s, pw), lambda bi, pi: (bi, 0, pi)),
        out_shape=jax.ShapeDtypeStruct((b, s, ATT_WIDTH), F32),
        scratch_shapes=[
            pltpu.VMEM((2, nq, pw, qb), BF16),
            pltpu.VMEM((2, nb, qb), F32),
            pltpu.VMEM((2, 1, qb), F32),
            pltpu.VMEM((2, pw, qb), F32),
        ],
        compiler_params=pltpu.CompilerParams(
            dimension_semantics=("arbitrary", "arbitrary"),
            vmem_limit_bytes=VMEM_LIMIT_BYTES),
        name="prompt_attn",
    )(qt, kn, vtb, km)


def _mix_out_ab(att, sg, ml, wo_ref):
    ma = (sg.astype(F32) * att).astype(BF16)
    return _dot(ma, wo_ref[:ATT_WIDTH, :]) + _dot(ml.astype(BF16), wo_ref[ATT_WIDTH:, :])


def _gmlp_in(y0, npre_ref, wi_ref, lg_ref, lb_ref):
    xn = _rms_norm(y0, npre_ref[...])
    pr = _dot(xn.astype(BF16), wi_ref[...])
    u = _gelu_tanh(pr[:, :GMLP_WIDTH])
    v = _layer_norm(_gelu_tanh(pr[:, GMLP_WIDTH:2 * GMLP_WIDTH]), lg_ref[...], lb_ref[...])
    g = pr[:, 2 * GMLP_WIDTH:]
    return u, v, g


def _prompt_out_kernel(x_ref, att_ref, sg_ref, ml_ref, np0_ref, wo_ref, npre_ref, wi_ref,
                       lg_ref, lb_ref, ws_ref, bst_ref, wc_ref, np1_ref, y_ref, mix_ref):
    tm = x_ref.shape[0]
    op = _mix_out_ab(att_ref[...], sg_ref[...], ml_ref[...], wo_ref)
    y0 = x_ref[...] + _rms_norm(op, np0_ref[...])
    u, v, g = _gmlp_in(y0, npre_ref, wi_ref, lg_ref, lb_ref)
    vb = v.astype(BF16)
    t_out = lax.broadcasted_iota(jnp.int32, (CHUNK, CHUNK), 0)
    t_in = lax.broadcasted_iota(jnp.int32, (CHUNK, CHUNK), 1)
    for gi in range(GMLP_GROUPS):
        wm = jnp.where(t_in <= t_out, ws_ref[gi], 0.0).astype(BF16)
        cols = slice(gi * GMLP_GROUP_DIM, (gi + 1) * GMLP_GROUP_DIM)
        for c in range(tm // CHUNK):
            rows = slice(c * CHUNK, (c + 1) * CHUNK)
            mix_ref[rows, cols] = _dot(wm, vb[rows, cols]) + bst_ref[:, cols]
    z = _silu(g) * (u * mix_ref[...])
    op1 = _dot(z.astype(BF16), wc_ref[...])
    y_ref[...] = y0 + _rms_norm(op1, np1_ref[...])


def _prompt_out(x2, att2, sg2, ml2, np0, wo, npre1, wi, lg, lb, ws, bst, wc, np1, tm):
    n, d = x2.shape
    const = lambda shape: pl.BlockSpec(shape, lambda i: (0,) * len(shape))
    rows = lambda width: pl.BlockSpec((tm, width), lambda i: (i, 0))
    return pl.pallas_call(
        _prompt_out_kernel,
        grid=(n // tm,),
        in_specs=[rows(d), rows(ATT_WIDTH), rows(ATT_WIDTH), rows(LRU_WIDTH),
                  const(np0.shape), const(wo.shape), const(npre1.shape), const(wi.shape),
                  const(lg.shape), const(lb.shape), const(ws.shape), const(bst.shape),
                  const(wc.shape), const(np1.shape)],
        out_specs=rows(d),
        out_shape=jax.ShapeDtypeStruct((n, d), F32),
        scratch_shapes=[pltpu.VMEM((tm, GMLP_WIDTH), F32)],
        compiler_params=pltpu.CompilerParams(
            dimension_semantics=("arbitrary",),
            vmem_limit_bytes=VMEM_LIMIT_BYTES),
        name="prompt_out",
    )(x2, att2, sg2, ml2, np0, wo, npre1, wi, lg, lb, ws, bst, wc, np1)


def _sample_in_kernel(x_ref, g_ref, whi_ref, wlo_ref, cw_ref, cb_ref, wa_ref, wx_ref,
                      ba_ref, bx_ref, lam_ref, h0_ref, st_ref,
                      q_ref, k_ref, v_ref, sg_ref, ml_ref, hs_ref, xl_ref):
    n = x_ref.shape[0]
    t_len = V7X_SUBLANES
    w = LRU_WIDTH
    xn = _rms_norm(x_ref[...], g_ref[...])
    qk, rest = _in_proj_ab(xn, whi_ref, wlo_ref)
    q_ref[...] = qk[:, :ATT_WIDTH]
    k_ref[...] = qk[:, ATT_WIDTH:]
    v_ref[...] = rest[:, :ATT_WIDTH]
    g_att = rest[:, ATT_WIDTH:2 * ATT_WIDTH]
    x_lru = rest[:, 2 * ATT_WIDTH:2 * ATT_WIDTH + w]
    g_lru = rest[:, 2 * ATT_WIDTH + w:]
    sg_ref[...] = _silu(g_att)
    xl_ref[...] = x_lru

    tok = lax.broadcasted_iota(jnp.int32, (n, w), 0) % t_len
    st = st_ref[...]
    xc = cw_ref[CONV_WIDTH - 1:CONV_WIDTH, :] * x_lru + cb_ref[...]
    for back in range(1, CONV_WIDTH):
        prev = jnp.where(tok >= back, pltpu.roll(x_lru, back, 0),
                         pltpu.roll(st, n - t_len + back, 0))
        xc = xc + cw_ref[CONV_WIDTH - 1 - back:CONV_WIDTH - back, :] * prev

    a, u = _lru_coeffs(xc, wa_ref, wx_ref, ba_ref, bx_ref, lam_ref)
    for d in (1, 2, 4):
        keep = tok >= d
        u = jnp.where(keep, a * pltpu.roll(u, d, 0) + u, u)
        a = jnp.where(keep, a * pltpu.roll(a, d, 0), a)
    hs = a * h0_ref[...] + u
    hs_ref[...] = hs
    ml_ref[...] = _silu(g_lru) * hs


def _sample_in(xs, g_pre, whi, wlo, cw, cb, wa, wx, ba, bx, lam, h0rep, stpad):
    n, d = xs.shape
    w = LRU_WIDTH
    args = (xs, g_pre, whi, wlo, cw, cb, wa, wx, ba, bx, lam, h0rep, stpad)
    full = lambda a: pl.BlockSpec(a.shape, lambda i: (0,) * a.ndim)
    outs = [jax.ShapeDtypeStruct((n, ATT_WIDTH), F32)] * 4 + [jax.ShapeDtypeStruct((n, w), F32)] * 3
    return pl.pallas_call(
        _sample_in_kernel,
        grid=(1,),
        in_specs=[full(a) for a in args],
        out_specs=tuple(pl.BlockSpec(o.shape, lambda i: (0, 0)) for o in outs),
        out_shape=tuple(outs),
        compiler_params=pltpu.CompilerParams(
            dimension_semantics=("arbitrary",), vmem_limit_bytes=VMEM_LIMIT_BYTES),
        name="sample_in",
    )(*args)


def _sample_select_kernel(pt_ref, q_ref, *refs, pages_per_step, n_blocks):
    k_refs = refs[:pages_per_step]
    idx_ref = refs[pages_per_step]
    kmt = refs[pages_per_step + 1]
    c = pl.program_id(1)
    bp = MOBA_BLOCK // PAGE_SIZE
    lane3 = lax.broadcasted_iota(jnp.int32, kmt.shape, 2)

    @pl.when(c == 0)
    def _():
        kmt[...] = jnp.zeros(kmt.shape, F32)

    for i in range(pages_per_step // bp):
        tot = k_refs[bp * i][...]
        for pg in range(1, bp):
            tot = tot + k_refs[bp * i + pg][...]
        col = jnp.sum(tot, axis=-1, keepdims=True) * (1.0 / MOBA_BLOCK)
        blk = c * (pages_per_step // bp) + i
        kmt[...] = jnp.where(lane3 == blk, col, kmt[...])

    @pl.when(c == pl.num_programs(1) - 1)
    def _():
        qv = q_ref[...]
        t_len = qv.shape[0]
        lane = lax.broadcasted_iota(jnp.int32, (t_len, V7X_LANES), 1)
        lane_pair = lax.broadcasted_iota(jnp.int32, (t_len, 2 * HEAD_DIM), 1)
        for p in range(HEAD_PAIRS):
            qp = qv[:, p * 2 * HEAD_DIM:(p + 1) * 2 * HEAD_DIM]
            kmp = kmt[2 * p:2 * p + 2].reshape(2 * HEAD_DIM, V7X_LANES)
            for hh in range(2):
                qm = jnp.where((lane_pair < HEAD_DIM) == (hh == 0), qp, 0.0)
                gate = _dot_exact(qm, kmp)
                gate = jnp.where(lane < n_blocks, gate, NEG_INF)
                out = jnp.zeros((t_len, V7X_LANES), jnp.int32)
                for r in range(MOBA_TOPK):
                    mx = jnp.max(gate, axis=-1, keepdims=True)
                    ix = jnp.min(jnp.where(gate == mx, lane, V7X_LANES), axis=-1, keepdims=True)
                    out = jnp.where(lane == r, ix, out)
                    gate = jnp.where(lane == ix, NEG_INF, gate)
                idx_ref[2 * p + hh] = out


def _sample_select(page_table, q_s, cache_t, layer, pages_per_step):
    db, n_pages = page_table.shape
    t_len = q_s.shape[0] // db
    n_blocks = n_pages * PAGE_SIZE // MOBA_BLOCK
    assert n_blocks <= V7X_LANES and n_pages % pages_per_step == 0

    def page_spec(i):
        return pl.BlockSpec(
            (None, None, ATT_HEADS, HEAD_DIM, PAGE_SIZE),
            lambda b, c, pt: (layer, pt[b * n_pages + c * pages_per_step + i], 0, 0, 0))

    grid_spec = pltpu.PrefetchScalarGridSpec(
        num_scalar_prefetch=1,
        grid=(db, n_pages // pages_per_step),
        in_specs=[pl.BlockSpec((t_len, ATT_WIDTH), lambda b, c, pt: (b, 0))]
                 + [page_spec(i) for i in range(pages_per_step)],
        out_specs=pl.BlockSpec((None, ATT_HEADS, t_len, V7X_LANES), lambda b, c, pt: (b, 0, 0, 0)),
        scratch_shapes=[pltpu.VMEM((ATT_HEADS, HEAD_DIM, V7X_LANES), F32)],
    )
    return pl.pallas_call(
        functools.partial(_sample_select_kernel, pages_per_step=pages_per_step, n_blocks=n_blocks),
        grid_spec=grid_spec,
        out_shape=jax.ShapeDtypeStruct((db, ATT_HEADS, t_len, V7X_LANES), jnp.int32),
        compiler_params=pltpu.CompilerParams(
            dimension_semantics=("arbitrary", "arbitrary"), vmem_limit_bytes=VMEM_LIMIT_BYTES),
        name="sample_select",
    )(page_table.reshape(-1), q_s, *([cache_t] * pages_per_step))


def _sample_attn_kernel(ph_ref, qt_ref, kt_ref, vt_ref, *refs, t_len):
    n_sel = MOBA_TOPK * (MOBA_BLOCK // PAGE_SIZE)
    k_refs = refs[:t_len * n_sel]
    v_refs = refs[t_len * n_sel:2 * t_len * n_sel]
    o_ref = refs[2 * t_len * n_sel]
    ktn = kt_ref[...]
    vtn = vt_ref[...]
    tok = lax.broadcasted_iota(jnp.int32, (1, t_len), 1)
    for t in range(t_len):
        qc = qt_ref[:, t:t + 1] * ATT_SCALE
        s_own = jnp.sum(ktn * qc, axis=0, keepdims=True)
        s_own = jnp.where(tok <= t, s_own, NEG_INF)
        s_sel = [jnp.sum(k_refs[t * n_sel + i][...] * qc, axis=0, keepdims=True)
                 for i in range(n_sel)]
        m = jnp.max(s_own, axis=-1, keepdims=True)
        for s in s_sel:
            m = jnp.maximum(m, jnp.max(s, axis=-1, keepdims=True))
        p_own = jnp.exp(s_own - m)
        den = jnp.sum(p_own, axis=-1, keepdims=True)
        acc = None
        for i, s in enumerate(s_sel):
            p = jnp.exp(s - m)
            den = den + jnp.sum(p, axis=-1, keepdims=True)
            pv = v_refs[t * n_sel + i][...] * p
            acc = pv if acc is None else acc + pv
        o = jnp.sum(acc, axis=-1, keepdims=True) + jnp.sum(vtn * p_own, axis=-1, keepdims=True)
        o_ref[:, t:t + 1] = o / den


def _sample_attn(phys, qt_s, kt_s, vt_s, cache_kt, cache_vt, layer):
    db, _, t_len = qt_s.shape
    n_sel = MOBA_TOPK * (MOBA_BLOCK // PAGE_SIZE)
    per_bh = t_len * n_sel

    def page_spec(i):
        return pl.BlockSpec(
            (None, None, None, HEAD_DIM, PAGE_SIZE),
            lambda b, h, ph: (layer, ph[(b * ATT_HEADS + h) * per_bh + i], h, 0, 0))

    new_spec = pl.BlockSpec((None, HEAD_DIM, t_len), lambda b, h, ph: (b, h, 0))
    grid_spec = pltpu.PrefetchScalarGridSpec(
        num_scalar_prefetch=1,
        grid=(db, ATT_HEADS),
        in_specs=[new_spec, new_spec, new_spec] + [page_spec(i) for i in range(per_bh)] * 2,
        out_specs=new_spec,
    )
    return pl.pallas_call(
        functools.partial(_sample_attn_kernel, t_len=t_len),
        grid_spec=grid_spec,
        out_shape=jax.ShapeDtypeStruct((db, ATT_WIDTH, t_len), F32),
        compiler_params=pltpu.CompilerParams(
            dimension_semantics=("arbitrary", "arbitrary"), vmem_limit_bytes=VMEM_LIMIT_BYTES),
        name="sample_attn",
    )(phys, qt_s, kt_s, vt_s, *([cache_kt] * per_bh), *([cache_vt] * per_bh))


def _sample_out_kernel(x_ref, att_ref, sg_ref, ml_ref, np0_ref, wo_ref, npre_ref, wi_ref,
                       lg_ref, lb_ref, cd_ref, bst_ref, wc_ref, np1_ref, y_ref, gv_ref):
    n = x_ref.shape[0]
    t_len = cd_ref.shape[1]
    reps = n // t_len
    op = _mix_out_ab(att_ref[...], sg_ref[...], ml_ref[...], wo_ref)
    y0 = x_ref[...] + _rms_norm(op, np0_ref[...])
    u, v, g = _gmlp_in(y0, npre_ref, wi_ref, lg_ref, lb_ref)
    gv_ref[...] = v
    tile = lambda tab: jnp.concatenate([tab] * reps, axis=0)
    mix = tile(bst_ref[...]) + tile(cd_ref[0]) * v
    for d in range(1, t_len):
        mix = mix + tile(cd_ref[d]) * pltpu.roll(v, d, 0)
    z = _silu(g) * (u * mix)
    op1 = _dot(z.astype(BF16), wc_ref[...])
    y_ref[...] = y0 + _rms_norm(op1, np1_ref[...])


def _sample_out(xs, att, sg, ml, np0, wo, npre1, wi, lg, lb, cd, bst8, wc, np1):
    n, d = xs.shape
    args = (xs, att, sg, ml, np0, wo, npre1, wi, lg, lb, cd, bst8, wc, np1)
    full = lambda a: pl.BlockSpec(a.shape, lambda i: (0,) * a.ndim)
    outs = (jax.ShapeDtypeStruct((n, d), F32), jax.ShapeDtypeStruct((n, GMLP_WIDTH), F32))
    return pl.pallas_call(
        _sample_out_kernel,
        grid=(1,),
        in_specs=[full(a) for a in args],
        out_specs=tuple(pl.BlockSpec(o.shape, lambda i: (0, 0)) for o in outs),
        out_shape=outs,
        compiler_params=pltpu.CompilerParams(
            dimension_semantics=("arbitrary",), vmem_limit_bytes=VMEM_LIMIT_BYTES),
        name="sample_out",
    )(*args)


def _block_diag(wh):
    h, n, _ = wh.shape
    eye = jnp.eye(h, dtype=wh.dtype)
    return jnp.einsum("hij,hg->higj", wh, eye).reshape(h * n, h * n)


def kernel(x_prompt, x_sample, cache_k, cache_v, page_table, state_lru_h, state_conv, norm_pre, norm_post, w_in_ab, conv_w, conv_b, lru_wa, lru_ba, lru_wx, lru_bx, lru_lambda, w_out_ab, w_in_c, c_ln_g, c_ln_b, c_ws, c_bs, w_out_c):
    b, s, d = x_prompt.shape
    db, t_len, _ = x_sample.shape
    n_pages = page_table.shape[1]
    assert norm_pre.shape[0] == 2 and w_in_ab.shape[0] == 1 and w_in_c.shape[0] == 1
    assert s % (2 * MOBA_BLOCK) == 0 and t_len == V7X_SUBLANES
    assert (n_pages * PAGE_SIZE) % MOBA_BLOCK == 0 and t_len <= CHUNK
    assert cache_k.shape[2:] == (PAGE_SIZE, ATT_HEADS, HEAD_DIM)
    w = LRU_WIDTH
    row = lambda vec: vec.reshape(1, -1)

    whi = w_in_ab[0].astype(BF16)
    wq = w_in_ab[0][:, :2 * ATT_WIDTH]
    wlo = (wq - wq.astype(BF16).astype(F32)).astype(BF16)
    wa = _block_diag(lru_wa[0]).astype(BF16)
    wx = _block_diag(lru_wx[0]).astype(BF16)
    lru_args = (conv_w[0], row(conv_b[0]), wa, wx, row(lru_ba[0]), row(lru_bx[0]), row(lru_lambda[0]))
    wo = w_out_ab[0].astype(BF16)
    wi = w_in_c[0].astype(BF16)
    wc = w_out_c[0].astype(BF16)
    np0, np1 = row(norm_post[0]), row(norm_post[1])
    npre0, npre1 = row(norm_pre[0]), row(norm_pre[1])
    lg, lb = row(c_ln_g[0]), row(c_ln_b[0])
    bst = jnp.repeat(c_bs[0].T, GMLP_GROUP_DIM, axis=1)

    (qt, kt, vt, vtb, kn, km, sg, ml, h_last, conv_last) = _prompt_in(
        x_prompt, npre0, whi, wlo, *lru_args, tm=2 * MOBA_BLOCK)
    att = _prompt_attn(qt, kn, vtb, km)
    y_prompt = _prompt_out(
        x_prompt.reshape(b * s, d), att.reshape(b * s, ATT_WIDTH), sg.reshape(b * s, ATT_WIDTH),
        ml.reshape(b * s, w), np0, wo, npre1, wi, lg, lb, c_ws[0], bst, wc, np1,
        tm=MOBA_BLOCK).reshape(b, s, d)
    heads_last = lambda xt: xt.reshape(b, ATT_HEADS, HEAD_DIM, s).transpose(0, 3, 1, 2)[None]
    k_prompt, v_prompt = heads_last(kt), heads_last(vt)

    n = db * t_len
    xs = x_sample.reshape(n, d)
    h0rep = jnp.repeat(state_lru_h[0], t_len, axis=0)
    stpad = jnp.pad(state_conv[0], ((0, 0), (t_len - (CONV_WIDTH - 1), 0), (0, 0))).reshape(n, w)
    q_s, k_s, v_s, sg_s, ml_s, hs_s, xl_s = _sample_in(xs, npre0, whi, wlo, *lru_args, h0rep, stpad)

    cache_kt = cache_k.transpose(0, 1, 3, 4, 2)
    cache_vt = cache_v.transpose(0, 1, 3, 4, 2)
    idx = _sample_select(page_table, q_s, cache_kt, 0, pages_per_step=16)
    bp = MOBA_BLOCK // PAGE_SIZE
    logical = idx[..., :MOBA_TOPK, None] * bp + jnp.arange(bp, dtype=jnp.int32)
    phys = jnp.take_along_axis(page_table, logical.reshape(db, -1), axis=1).reshape(-1)
    tok_last = lambda a: a.reshape(db, t_len, ATT_WIDTH).transpose(0, 2, 1)
    att_t = _sample_attn(phys, tok_last(q_s), tok_last(k_s), tok_last(v_s), cache_kt, cache_vt, 0)
    att_s = att_t.transpose(0, 2, 1).reshape(n, ATT_WIDTH)

    ws8 = c_ws[0][:, :t_len, :t_len]
    tt = jnp.arange(t_len)
    diag = lambda dd: jnp.where(tt >= dd, ws8[:, tt, jnp.maximum(tt - dd, 0)], 0.0)
    cd = jnp.stack([jnp.repeat(diag(dd).T, GMLP_GROUP_DIM, axis=1) for dd in range(t_len)])
    y_s, gv_s = _sample_out(xs, att_s, sg_s, ml_s, np0, wo, npre1, wi, lg, lb, cd, bst[:t_len],
                            wc, np1)

    per_req = lambda a, width: a.reshape(db, t_len, width)
    return (y_prompt, y_s.reshape(db, t_len, d), k_prompt, v_prompt,
            h_last.reshape(1, b, w), conv_last[None],
            k_s.reshape(1, db, t_len, ATT_HEADS, HEAD_DIM), v_s.reshape(1, db, t_len, ATT_HEADS, HEAD_DIM),
            per_req(hs_s, w)[:, t_len - 1][None], per_req(xl_s, w)[:, t_len - (CONV_WIDTH - 1):][None],
            per_req(gv_s, GMLP_WIDTH)[None])
```

```python
import functools

import jax
import jax.numpy as jnp
from jax import lax
from jax.experimental import pallas as pl
from jax.experimental.pallas import tpu as pltpu

ATT_HEADS = 8
HEAD_DIM = 64
ATT_WIDTH = ATT_HEADS * HEAD_DIM
HEAD_PAIRS = ATT_HEADS // 2
MOBA_BLOCK = 256
MOBA_TOPK = 3
QUERY_BLOCK = 128
LRU_WIDTH = 512
CONV_WIDTH = 4
LRU_C = 8.0
GMLP_WIDTH = 1024
GMLP_GROUPS = 8
GMLP_GROUP_DIM = GMLP_WIDTH // GMLP_GROUPS
CHUNK = 128
PAGE_SIZE = 128
NORM_EPS = 1e-6
ATT_SCALE = HEAD_DIM ** -0.5
LOG2_E = 1.4426950408889634

V7X_LANES = 128
V7X_SUBLANES = 8
VMEM_LIMIT_BYTES = 56 * 1024 * 1024

F32 = jnp.float32
BF16 = jnp.bfloat16
NEG_INF = float("-inf")
POS_INF = float("inf")


def _rms_norm(x, g):
    return x * lax.rsqrt(jnp.mean(x * x, axis=-1, keepdims=True) + NORM_EPS) * g


def _layer_norm(x, g, b):
    mu = jnp.mean(x, axis=-1, keepdims=True)
    xc = x - mu
    var = jnp.mean(xc * xc, axis=-1, keepdims=True)
    return xc * lax.rsqrt(var + NORM_EPS) * g + b


def _sigmoid(x):
    return 1.0 / (1.0 + jnp.exp(-x))


def _silu(x):
    return x * _sigmoid(x)


def _gelu_tanh(x):
    c = 0.7978845608028654
    return 0.5 * x * (1.0 + jnp.tanh(c * (x + 0.044715 * (x * x * x))))


def _softplus(x):
    return jnp.maximum(x, 0.0) + jnp.log1p(jnp.exp(-jnp.abs(x)))


def _dot(a, b):
    return jnp.dot(a, b, preferred_element_type=F32)


def _dot_exact(a, b):
    return jnp.dot(a, b, preferred_element_type=F32, precision=lax.Precision.HIGHEST)


def _split_bf16(x):
    hi = x.astype(BF16)
    lo = (x - hi.astype(F32)).astype(BF16)
    return hi, lo


def _in_proj_ab(xn, whi_ref, wlo_ref):
    qk_w = 2 * ATT_WIDTH
    xh, xl = _split_bf16(xn)
    qk = (_dot(xh, whi_ref[:, :qk_w]) + _dot(xh, wlo_ref[...])
          + _dot(xl, whi_ref[:, :qk_w]))
    rest = _dot(xh, whi_ref[:, qk_w:])
    return qk, rest


def _lru_coeffs(xc, wa_ref, wx_ref, ba_ref, bx_ref, lam_ref):
    xcb = xc.astype(BF16)
    r = _sigmoid(_dot(xcb, wa_ref[...]) + ba_ref[...])
    i = _sigmoid(_dot(xcb, wx_ref[...]) + bx_ref[...])
    log_a = (-LRU_C) * r * _softplus(-lam_ref[...])
    a = jnp.exp(log_a)
    u = jnp.sqrt(-jnp.tanh(log_a) * (a * a + 1.0)) * (i * xc)
    return a, u


def _top3_rows(gate, blk, n_valid):
    nb = gate.shape[0]
    sel = jnp.zeros(gate.shape, F32)
    for r in range(MOBA_TOPK):
        mx = jnp.max(gate, axis=0, keepdims=True)
        ix = jnp.min(jnp.where(gate == mx, blk, nb), axis=0, keepdims=True)
        hit = blk == ix
        sel = jnp.maximum(sel, jnp.where(hit, (r < n_valid).astype(F32), 0.0))
        gate = jnp.where(hit, NEG_INF, gate)
    return sel


def _prompt_in_kernel(x_ref, g_ref, whi_ref, wlo_ref, cw_ref, cb_ref, wa_ref, wx_ref,
                      ba_ref, bx_ref, lam_ref,
                      qt_ref, kt_ref, vt_ref, vtb_ref, kn_ref, km_ref, sg_ref, ml_ref,
                      hl_ref, cl_ref,
                      xbuf, hcar, abuf, ubuf):
    t = pl.program_id(1)
    tm = x_ref.shape[0]
    w = LRU_WIDTH

    @pl.when(t == 0)
    def _():
        xbuf[0:V7X_SUBLANES, :] = jnp.zeros((V7X_SUBLANES, w), F32)
        hcar[...] = jnp.zeros(hcar.shape, F32)

    xn = _rms_norm(x_ref[...], g_ref[...])
    qk, rest = _in_proj_ab(xn, whi_ref, wlo_ref)
    q = qk[:, :ATT_WIDTH]
    k = qk[:, ATT_WIDTH:]
    v = rest[:, :ATT_WIDTH]
    g_att = rest[:, ATT_WIDTH:2 * ATT_WIDTH]
    x_lru = rest[:, 2 * ATT_WIDTH:2 * ATT_WIDTH + w]
    g_lru = rest[:, 2 * ATT_WIDTH + w:]

    for s in range(tm // QUERY_BLOCK):
        rows = slice(s * QUERY_BLOCK, (s + 1) * QUERY_BLOCK)
        qt_ref[s] = q[rows, :].T
    kt_ref[...] = k.T
    vt = v.T
    vt_ref[...] = vt
    kn_ref[...] = k.astype(BF16)
    for s in range(tm // MOBA_BLOCK):
        vtb_ref[s] = vt[:, s * MOBA_BLOCK:(s + 1) * MOBA_BLOCK].astype(BF16)
        km_ref[pl.ds(t * (tm // MOBA_BLOCK) + s, 1), :] = jnp.mean(
            k[s * MOBA_BLOCK:(s + 1) * MOBA_BLOCK, :], axis=0, keepdims=True)
    sg_ref[...] = _silu(g_att).astype(BF16)

    xbuf[V7X_SUBLANES:, :] = x_lru
    xc = cw_ref[CONV_WIDTH - 1:CONV_WIDTH, :] * x_lru + cb_ref[...]
    for back in range(1, CONV_WIDTH):
        xc = xc + (cw_ref[CONV_WIDTH - 1 - back:CONV_WIDTH - back, :]
                   * xbuf[pl.ds(V7X_SUBLANES - back, tm), :])
    xbuf[0:V7X_SUBLANES, :] = x_lru[tm - V7X_SUBLANES:, :]

    a, u = _lru_coeffs(xc, wa_ref, wx_ref, ba_ref, bx_ref, lam_ref)
    abuf[...] = a
    ubuf[...] = u

    row = lax.broadcasted_iota(jnp.int32, (V7X_SUBLANES, w), 0)

    def group(gi, h):
        r0 = pl.multiple_of(gi * V7X_SUBLANES, V7X_SUBLANES)
        ag = abuf[pl.ds(r0, V7X_SUBLANES), :]
        ug = ubuf[pl.ds(r0, V7X_SUBLANES), :]
        for d in (1, 2, 4):
            keep = row >= d
            ug = jnp.where(keep, ag * pltpu.roll(ug, d, 0) + ug, ug)
            ag = jnp.where(keep, ag * pltpu.roll(ag, d, 0), ag)
        hg = ag * h + ug
        ubuf[pl.ds(r0, V7X_SUBLANES), :] = hg
        return jnp.broadcast_to(hg[V7X_SUBLANES - 1:, :], (V7X_SUBLANES, w))

    hcar[...] = lax.fori_loop(0, tm // V7X_SUBLANES, group, hcar[...])
    hs = ubuf[...]
    ml_ref[...] = (_silu(g_lru) * hs).astype(BF16)

    @pl.when(t == pl.num_programs(1) - 1)
    def _():
        hl_ref[...] = hs[tm - 1:, :]
        cl_ref[...] = x_lru[tm - (CONV_WIDTH - 1):, :]


def _prompt_in(x, g_pre, whi, wlo, cw, cb, wa, wx, ba, bx, lam, tm):
    b, s, d = x.shape
    w = LRU_WIDTH
    nb = s // MOBA_BLOCK
    nq = s // QUERY_BLOCK
    const = lambda shape: pl.BlockSpec(shape, lambda bi, ti: (0,) * len(shape))
    out_shape = (
        jax.ShapeDtypeStruct((b, nq, ATT_WIDTH, QUERY_BLOCK), F32),
        jax.ShapeDtypeStruct((b, ATT_WIDTH, s), F32),
        jax.ShapeDtypeStruct((b, ATT_WIDTH, s), F32),
        jax.ShapeDtypeStruct((b, nb, ATT_WIDTH, MOBA_BLOCK), BF16),
        jax.ShapeDtypeStruct((b, s, ATT_WIDTH), BF16),
        jax.ShapeDtypeStruct((b, nb, ATT_WIDTH), F32),
        jax.ShapeDtypeStruct((b, s, ATT_WIDTH), BF16),
        jax.ShapeDtypeStruct((b, s, w), BF16),
        jax.ShapeDtypeStruct((b, 1, w), F32),
        jax.ShapeDtypeStruct((b, CONV_WIDTH - 1, w), F32),
    )
    out_specs = (
        pl.BlockSpec((None, tm // QUERY_BLOCK, ATT_WIDTH, QUERY_BLOCK), lambda bi, ti: (bi, ti, 0, 0)),
        pl.BlockSpec((None, ATT_WIDTH, tm), lambda bi, ti: (bi, 0, ti)),
        pl.BlockSpec((None, ATT_WIDTH, tm), lambda bi, ti: (bi, 0, ti)),
        pl.BlockSpec((None, tm // MOBA_BLOCK, ATT_WIDTH, MOBA_BLOCK), lambda bi, ti: (bi, ti, 0, 0)),
        pl.BlockSpec((None, tm, ATT_WIDTH), lambda bi, ti: (bi, ti, 0)),
        pl.BlockSpec((None, nb, ATT_WIDTH), lambda bi, ti: (bi, 0, 0)),
        pl.BlockSpec((None, tm, ATT_WIDTH), lambda bi, ti: (bi, ti, 0)),
        pl.BlockSpec((None, tm, w), lambda bi, ti: (bi, ti, 0)),
        pl.BlockSpec((None, 1, w), lambda bi, ti: (bi, 0, 0)),
        pl.BlockSpec((None, CONV_WIDTH - 1, w), lambda bi, ti: (bi, 0, 0)),
    )
    in_specs = [
        pl.BlockSpec((None, tm, d), lambda bi, ti: (bi, ti, 0)),
        const((1, d)), const(whi.shape), const(wlo.shape), const(cw.shape), const(cb.shape),
        const(wa.shape), const(wx.shape), const(ba.shape), const(bx.shape), const(lam.shape),
    ]
    return pl.pallas_call(
        _prompt_in_kernel,
        grid=(b, s // tm),
        in_specs=in_specs,
        out_specs=out_specs,
        out_shape=out_shape,
        scratch_shapes=[
            pltpu.VMEM((tm + V7X_SUBLANES, w), F32),
            pltpu.VMEM((V7X_SUBLANES, w), F32),
            pltpu.VMEM((tm, w), F32),
            pltpu.VMEM((tm, w), F32),
        ],
        compiler_params=pltpu.CompilerParams(
            dimension_semantics=("arbitrary", "arbitrary"),
            vmem_limit_bytes=VMEM_LIMIT_BYTES),
        name="prompt_in",
    )(x, g_pre, whi, wlo, cw, cb, wa, wx, ba, bx, lam)


def _prompt_attn_kernel(qt_ref, kn_ref, vtb_ref, km_ref, o_ref, w_ref, sel_ref,
                        s0a_ref, s0b_ref, s1a_ref, s1b_ref, acc_ref):
    nsb = vtb_ref.shape[0]
    nb = km_ref.shape[0]
    tq = MOBA_BLOCK
    hd = HEAD_DIM
    ones_rows = 2 * V7X_SUBLANES
    first = lax.broadcasted_iota(jnp.int32, (2 * hd, tq), 0) < hd
    kmp = km_ref[...]
    blk = lax.broadcasted_iota(jnp.int32, (nb, tq), 0)
    ones_tile = jnp.ones((ones_rows, tq), BF16)
    causal = (lax.broadcasted_iota(jnp.int32, (tq, tq), 0)
              <= lax.broadcasted_iota(jnp.int32, (tq, tq), 1))

    def stage(sb, c):
        qt = jnp.concatenate([qt_ref[2 * sb], qt_ref[2 * sb + 1]], axis=1)
        for h in range(2):
            qth = jnp.where(first, qt, 0.0) if h == 0 else jnp.where(first, 0.0, qt)
            gate = _dot_exact(kmp, qth)
            gate = jnp.where(blk < sb, gate, NEG_INF)
            sel_ref[sb, h] = _top3_rows(gate, blk, sb)
            w_ref[sb, h] = (qth * (ATT_SCALE * LOG2_E)).astype(BF16)
        return c

    lax.fori_loop(0, nsb, stage, 0)

    def v_lhs(j, h):
        return jnp.concatenate([vtb_ref[j, h * hd:(h + 1) * hd, :], ones_tile], axis=0)

    def key_block(j):
        return kn_ref[pl.ds(pl.multiple_of(j * MOBA_BLOCK, MOBA_BLOCK), MOBA_BLOCK), :]

    heads = (0, 1)
    bufs = ((s0a_ref, s0b_ref), (s1a_ref, s1b_ref))

    def super_block(sb, c):
        last = sb - 1
        wq = tuple(w_ref[sb, h] for h in heads)
        for h in heads:
            bufs[h][0][...] = _dot(key_block(sb), wq[h])
        for h in heads:
            bufs[h][1][...] = _dot(key_block(0), wq[h])
        m_init = []
        for h in heads:
            s = jnp.where(causal, bufs[h][0][...], NEG_INF)
            m = jnp.max(s, axis=0, keepdims=True)
            acc_ref[h] = _dot(v_lhs(sb, h), jnp.exp2(s - m).astype(BF16))
            m_init.append(m)

        def past_block(j, live, cur, nxt, m_old):
            for h in heads:
                bufs[h][nxt][...] = _dot(key_block(jnp.minimum(j + 1, last)), wq[h])
            m_out = []
            for h in heads:
                cur_ref = bufs[h][cur]
                on = sel_ref[sb, h, pl.ds(j, 1), :] * live > 0.0
                cm = jnp.max(cur_ref[...], axis=0, keepdims=True)
                m_new = jnp.maximum(m_old[h], jnp.where(on, cm, NEG_INF))
                p = jnp.exp2(cur_ref[...] - jnp.where(on, m_new, POS_INF)).astype(BF16)
                acc_ref[h] = acc_ref[h] * jnp.exp2(m_old[h] - m_new) + _dot(v_lhs(j, h), p)
                m_out.append(m_new)
            return tuple(m_out)

        def past_pair(i, m_old):
            j0 = 2 * i
            j1 = jnp.minimum(j0 + 1, last)
            m_mid = past_block(j0, 1.0, 1, 0, m_old)
            return past_block(j1, (j0 + 1 < sb).astype(F32), 0, 1, m_mid)

        lax.fori_loop(0, (sb + 1) // 2, past_pair, tuple(m_init))
        outs = []
        for h in heads:
            a = acc_ref[h]
            outs.append(a[:hd, :] / a[hd:hd + 1, :])
        o_ref[pl.ds(pl.multiple_of(sb * tq, tq), tq), :] = jnp.concatenate(outs, axis=0).T
        return c

    lax.fori_loop(0, nsb, super_block, 0)


def _prompt_attn(qt, kn, vtb, km):
    b, nq, _, qb = qt.shape
    s = kn.shape[1]
    nb = km.shape[1]
    pw = 2 * HEAD_DIM
    tq = MOBA_BLOCK
    return pl.pallas_call(
        _prompt_attn_kernel,
        grid=(b, HEAD_PAIRS),
        in_specs=[
            pl.BlockSpec((None, nq, pw, qb), lambda bi, pi: (bi, 0, pi, 0)),
            pl.BlockSpec((None, s, pw), lambda bi, pi: (bi, 0, pi)),
            pl.BlockSpec((None, nb, pw, tq), lambda bi, pi: (bi, 0, pi, 0)),
            pl.BlockSpec((None, nb, pw), lambda bi, pi: (bi, 0, pi)),
        ],
        out_specs=pl.BlockSpec((None, s, pw), lambda bi, pi: (bi, 0, pi)),
        out_shape=jax.ShapeDtypeStruct((b, s, ATT_WIDTH), F32),
        scratch_shapes=[
            pltpu.VMEM((nb, 2, pw, tq), BF16),
            pltpu.VMEM((nb, 2, nb, tq), F32),
            pltpu.VMEM((tq, tq), F32), pltpu.VMEM((tq, tq), F32),
            pltpu.VMEM((tq, tq), F32), pltpu.VMEM((tq, tq), F32),
            pltpu.VMEM((2, HEAD_DIM + 2 * V7X_SUBLANES, tq), F32),
        ],
        compiler_params=pltpu.CompilerParams(
            dimension_semantics=("arbitrary", "arbitrary"),
            vmem_limit_bytes=VMEM_LIMIT_BYTES),
        name="prompt_attn",
    )(qt, kn, vtb, km)


def _mix_out_ab(att, sg, ml, wo_ref):
    ma = (sg.astype(F32) * att).astype(BF16)
    return _dot(ma, wo_ref[:ATT_WIDTH, :]) + _dot(ml.astype(BF16), wo_ref[ATT_WIDTH:, :])


def _gmlp_in(y0, npre_ref, wi_ref, lg_ref, lb_ref):
    xn = _rms_norm(y0, npre_ref[...])
    pr = _dot(xn.astype(BF16), wi_ref[...])
    u = _gelu_tanh(pr[:, :GMLP_WIDTH])
    v = _layer_norm(_gelu_tanh(pr[:, GMLP_WIDTH:2 * GMLP_WIDTH]), lg_ref[...], lb_ref[...])
    g = pr[:, 2 * GMLP_WIDTH:]
    return u, v, g


def _prompt_out_kernel(x_ref, att_ref, sg_ref, ml_ref, np0_ref, wo_ref, npre_ref, wi_ref,
                       lg_ref, lb_ref, ws_ref, bst_ref, wc_ref, np1_ref, y_ref, mix_ref):
    tm = x_ref.shape[0]
    op = _mix_out_ab(att_ref[...], sg_ref[...], ml_ref[...], wo_ref)
    y0 = x_ref[...] + _rms_norm(op, np0_ref[...])
    u, v, g = _gmlp_in(y0, npre_ref, wi_ref, lg_ref, lb_ref)
    vb = v.astype(BF16)
    t_out = lax.broadcasted_iota(jnp.int32, (CHUNK, CHUNK), 0)
    t_in = lax.broadcasted_iota(jnp.int32, (CHUNK, CHUNK), 1)
    for gi in range(GMLP_GROUPS):
        wm = jnp.where(t_in <= t_out, ws_ref[gi], 0.0).astype(BF16)
        cols = slice(gi * GMLP_GROUP_DIM, (gi + 1) * GMLP_GROUP_DIM)
        for c in range(tm // CHUNK):
            rows = slice(c * CHUNK, (c + 1) * CHUNK)
            mix_ref[rows, cols] = _dot(wm, vb[rows, cols]) + bst_ref[:, cols]
    z = _silu(g) * (u * mix_ref[...])
    op1 = _dot(z.astype(BF16), wc_ref[...])
    y_ref[...] = y0 + _rms_norm(op1, np1_ref[...])


def _prompt_out(x2, att2, sg2, ml2, np0, wo, npre1, wi, lg, lb, ws, bst, wc, np1, tm):
    n, d = x2.shape
    const = lambda shape: pl.BlockSpec(shape, lambda i: (0,) * len(shape))
    rows = lambda width: pl.BlockSpec((tm, width), lambda i: (i, 0))
    return pl.pallas_call(
        _prompt_out_kernel,
        grid=(n // tm,),
        in_specs=[rows(d), rows(ATT_WIDTH), rows(ATT_WIDTH), rows(LRU_WIDTH),
                  const(np0.shape), const(wo.shape), const(npre1.shape), const(wi.shape),
                  const(lg.shape), const(lb.shape), const(ws.shape), const(bst.shape),
                  const(wc.shape), const(np1.shape)],
        out_specs=rows(d),
        out_shape=jax.ShapeDtypeStruct((n, d), F32),
        scratch_shapes=[pltpu.VMEM((tm, GMLP_WIDTH), F32)],
        compiler_params=pltpu.CompilerParams(
            dimension_semantics=("arbitrary",),
            vmem_limit_bytes=VMEM_LIMIT_BYTES),
        name="prompt_out",
    )(x2, att2, sg2, ml2, np0, wo, npre1, wi, lg, lb, ws, bst, wc, np1)


def _sample_in_kernel(x_ref, g_ref, whi_ref, wlo_ref, cw_ref, cb_ref, wa_ref, wx_ref,
                      ba_ref, bx_ref, lam_ref, h0_ref, st_ref,
                      q_ref, k_ref, v_ref, sg_ref, ml_ref, hs_ref, xl_ref):
    n = x_ref.shape[0]
    t_len = V7X_SUBLANES
    w = LRU_WIDTH
    xn = _rms_norm(x_ref[...], g_ref[...])
    qk, rest = _in_proj_ab(xn, whi_ref, wlo_ref)
    q_ref[...] = qk[:, :ATT_WIDTH]
    k_ref[...] = qk[:, ATT_WIDTH:]
    v_ref[...] = rest[:, :ATT_WIDTH]
    g_att = rest[:, ATT_WIDTH:2 * ATT_WIDTH]
    x_lru = rest[:, 2 * ATT_WIDTH:2 * ATT_WIDTH + w]
    g_lru = rest[:, 2 * ATT_WIDTH + w:]
    sg_ref[...] = _silu(g_att)
    xl_ref[...] = x_lru

    tok = lax.broadcasted_iota(jnp.int32, (n, w), 0) % t_len
    st = st_ref[...]
    xc = cw_ref[CONV_WIDTH - 1:CONV_WIDTH, :] * x_lru + cb_ref[...]
    for back in range(1, CONV_WIDTH):
        prev = jnp.where(tok >= back, pltpu.roll(x_lru, back, 0),
                         pltpu.roll(st, n - t_len + back, 0))
        xc = xc + cw_ref[CONV_WIDTH - 1 - back:CONV_WIDTH - back, :] * prev

    a, u = _lru_coeffs(xc, wa_ref, wx_ref, ba_ref, bx_ref, lam_ref)
    for d in (1, 2, 4):
        keep = tok >= d
        u = jnp.where(keep, a * pltpu.roll(u, d, 0) + u, u)
        a = jnp.where(keep, a * pltpu.roll(a, d, 0), a)
    hs = a * h0_ref[...] + u
    hs_ref[...] = hs
    ml_ref[...] = _silu(g_lru) * hs


def _sample_in(xs, g_pre, whi, wlo, cw, cb, wa, wx, ba, bx, lam, h0rep, stpad):
    n, d = xs.shape
    w = LRU_WIDTH
    args = (xs, g_pre, whi, wlo, cw, cb, wa, wx, ba, bx, lam, h0rep, stpad)
    full = lambda a: pl.BlockSpec(a.shape, lambda i: (0,) * a.ndim)
    outs = [jax.ShapeDtypeStruct((n, ATT_WIDTH), F32)] * 4 + [jax.ShapeDtypeStruct((n, w), F32)] * 3
    return pl.pallas_call(
        _sample_in_kernel,
        grid=(1,),
        in_specs=[full(a) for a in args],
        out_specs=tuple(pl.BlockSpec(o.shape, lambda i: (0, 0)) for o in outs),
        out_shape=tuple(outs),
        compiler_params=pltpu.CompilerParams(
            dimension_semantics=("arbitrary",), vmem_limit_bytes=VMEM_LIMIT_BYTES),
        name="sample_in",
    )(*args)


def _sample_select_kernel(pt_ref, q_ref, *refs, pages_per_step, n_blocks):
    k_refs = refs[:pages_per_step]
    idx_ref = refs[pages_per_step]
    kmt = refs[pages_per_step + 1]
    c = pl.program_id(1)
    bp = MOBA_BLOCK // PAGE_SIZE
    lane3 = lax.broadcasted_iota(jnp.int32, kmt.shape, 2)

    @pl.when(c == 0)
    def _():
        kmt[...] = jnp.zeros(kmt.shape, F32)

    for i in range(pages_per_step // bp):
        tot = k_refs[bp * i][...]
        for pg in range(1, bp):
            tot = tot + k_refs[bp * i + pg][...]
        col = jnp.sum(tot, axis=-1, keepdims=True) * (1.0 / MOBA_BLOCK)
        blk = c * (pages_per_step // bp) + i
        kmt[...] = jnp.where(lane3 == blk, col, kmt[...])

    @pl.when(c == pl.num_programs(1) - 1)
    def _():
        qv = q_ref[...]
        t_len = qv.shape[0]
        lane = lax.broadcasted_iota(jnp.int32, (t_len, V7X_LANES), 1)
        lane_pair = lax.broadcasted_iota(jnp.int32, (t_len, 2 * HEAD_DIM), 1)
        for p in range(HEAD_PAIRS):
            qp = qv[:, p * 2 * HEAD_DIM:(p + 1) * 2 * HEAD_DIM]
            kmp = kmt[2 * p:2 * p + 2].reshape(2 * HEAD_DIM, V7X_LANES)
            for hh in range(2):
                qm = jnp.where((lane_pair < HEAD_DIM) == (hh == 0), qp, 0.0)
                gate = _dot_exact(qm, kmp)
                gate = jnp.where(lane < n_blocks, gate, NEG_INF)
                out = jnp.zeros((t_len, V7X_LANES), jnp.int32)
                for r in range(MOBA_TOPK):
                    mx = jnp.max(gate, axis=-1, keepdims=True)
                    ix = jnp.min(jnp.where(gate == mx, lane, V7X_LANES), axis=-1, keepdims=True)
                    out = jnp.where(lane == r, ix, out)
                    gate = jnp.where(lane == ix, NEG_INF, gate)
                idx_ref[2 * p + hh] = out


def _sample_select(page_table, q_s, cache_t, layer, pages_per_step):
    db, n_pages = page_table.shape
    t_len = q_s.shape[0] // db
    n_blocks = n_pages * PAGE_SIZE // MOBA_BLOCK
    assert n_blocks <= V7X_LANES and n_pages % pages_per_step == 0

    def page_spec(i):
        return pl.BlockSpec(
            (None, None, ATT_HEADS, HEAD_DIM, PAGE_SIZE),
            lambda b, c, pt: (layer, pt[b * n_pages + c * pages_per_step + i], 0, 0, 0))

    grid_spec = pltpu.PrefetchScalarGridSpec(
        num_scalar_prefetch=1,
        grid=(db, n_pages // pages_per_step),
        in_specs=[pl.BlockSpec((t_len, ATT_WIDTH), lambda b, c, pt: (b, 0))]
                 + [page_spec(i) for i in range(pages_per_step)],
        out_specs=pl.BlockSpec((None, ATT_HEADS, t_len, V7X_LANES), lambda b, c, pt: (b, 0, 0, 0)),
        scratch_shapes=[pltpu.VMEM((ATT_HEADS, HEAD_DIM, V7X_LANES), F32)],
    )
    return pl.pallas_call(
        functools.partial(_sample_select_kernel, pages_per_step=pages_per_step, n_blocks=n_blocks),
        grid_spec=grid_spec,
        out_shape=jax.ShapeDtypeStruct((db, ATT_HEADS, t_len, V7X_LANES), jnp.int32),
        compiler_params=pltpu.CompilerParams(
            dimension_semantics=("arbitrary", "arbitrary"), vmem_limit_bytes=VMEM_LIMIT_BYTES),
        name="sample_select",
    )(page_table.reshape(-1), q_s, *([cache_t] * pages_per_step))


def _sample_attn_kernel(pt_ref, ix_ref, qt_ref, kt_ref, vt_ref, *refs, t_len):
    n_sel = MOBA_TOPK * (MOBA_BLOCK // PAGE_SIZE)
    k_refs = refs[:t_len * n_sel]
    v_refs = refs[t_len * n_sel:2 * t_len * n_sel]
    o_ref = refs[2 * t_len * n_sel]
    ktn = kt_ref[...]
    vtn = vt_ref[...]
    tok = lax.broadcasted_iota(jnp.int32, (1, t_len), 1)
    for t in range(t_len):
        qc = qt_ref[:, t:t + 1] * ATT_SCALE
        s_own = jnp.sum(ktn * qc, axis=0, keepdims=True)
        s_own = jnp.where(tok <= t, s_own, NEG_INF)
        s_sel = [jnp.sum(k_refs[t * n_sel + i][...] * qc, axis=0, keepdims=True)
                 for i in range(n_sel)]
        m = jnp.max(s_own, axis=-1, keepdims=True)
        for s in s_sel:
            m = jnp.maximum(m, jnp.max(s, axis=-1, keepdims=True))
        p_own = jnp.exp(s_own - m)
        den = jnp.sum(p_own, axis=-1, keepdims=True)
        acc = None
        for i, s in enumerate(s_sel):
            p = jnp.exp(s - m)
            den = den + jnp.sum(p, axis=-1, keepdims=True)
            pv = v_refs[t * n_sel + i][...] * p
            acc = pv if acc is None else acc + pv
        o = jnp.sum(acc, axis=-1, keepdims=True) + jnp.sum(vtn * p_own, axis=-1, keepdims=True)
        o_ref[:, t:t + 1] = o / den


def _sample_attn(page_table, block_idx, qt_s, kt_s, vt_s, cache_kt, cache_vt, layer):
    db, _, t_len = qt_s.shape
    n_pages = page_table.shape[1]
    bp = MOBA_BLOCK // PAGE_SIZE
    n_sel = MOBA_TOPK * bp
    per_bh = t_len * n_sel

    def page_spec(i):
        pick, pg = divmod(i, bp)

        def index_map(b, h, pt, ix):
            blk = ix[(b * ATT_HEADS + h) * (t_len * MOBA_TOPK) + pick]
            return (layer, pt[b * n_pages + blk * bp + pg], h, 0, 0)

        return pl.BlockSpec((None, None, None, HEAD_DIM, PAGE_SIZE), index_map)

    new_spec = pl.BlockSpec((None, HEAD_DIM, t_len), lambda b, h, pt, ix: (b, h, 0))
    grid_spec = pltpu.PrefetchScalarGridSpec(
        num_scalar_prefetch=2,
        grid=(db, ATT_HEADS),
        in_specs=[new_spec, new_spec, new_spec] + [page_spec(i) for i in range(per_bh)] * 2,
        out_specs=new_spec,
    )
    return pl.pallas_call(
        functools.partial(_sample_attn_kernel, t_len=t_len),
        grid_spec=grid_spec,
        out_shape=jax.ShapeDtypeStruct((db, ATT_WIDTH, t_len), F32),
        compiler_params=pltpu.CompilerParams(
            dimension_semantics=("arbitrary", "arbitrary"), vmem_limit_bytes=VMEM_LIMIT_BYTES),
        name="sample_attn",
    )(page_table.reshape(-1), block_idx.reshape(-1), qt_s, kt_s, vt_s,
      *([cache_kt] * per_bh), *([cache_vt] * per_bh))


def _sample_out_kernel(x_ref, att_ref, sg_ref, ml_ref, np0_ref, wo_ref, npre_ref, wi_ref,
                       lg_ref, lb_ref, cd_ref, bst_ref, wc_ref, np1_ref, y_ref, gv_ref):
    n = x_ref.shape[0]
    t_len = cd_ref.shape[1]
    reps = n // t_len
    op = _mix_out_ab(att_ref[...], sg_ref[...], ml_ref[...], wo_ref)
    y0 = x_ref[...] + _rms_norm(op, np0_ref[...])
    u, v, g = _gmlp_in(y0, npre_ref, wi_ref, lg_ref, lb_ref)
    gv_ref[...] = v
    tile = lambda tab: jnp.concatenate([tab] * reps, axis=0)
    mix = tile(bst_ref[...]) + tile(cd_ref[0]) * v
    for d in range(1, t_len):
        mix = mix + tile(cd_ref[d]) * pltpu.roll(v, d, 0)
    z = _silu(g) * (u * mix)
    op1 = _dot(z.astype(BF16), wc_ref[...])
    y_ref[...] = y0 + _rms_norm(op1, np1_ref[...])


def _sample_out(xs, att, sg, ml, np0, wo, npre1, wi, lg, lb, cd, bst8, wc, np1):
    n, d = xs.shape
    args = (xs, att, sg, ml, np0, wo, npre1, wi, lg, lb, cd, bst8, wc, np1)
    full = lambda a: pl.BlockSpec(a.shape, lambda i: (0,) * a.ndim)
    outs = (jax.ShapeDtypeStruct((n, d), F32), jax.ShapeDtypeStruct((n, GMLP_WIDTH), F32))
    return pl.pallas_call(
        _sample_out_kernel,
        grid=(1,),
        in_specs=[full(a) for a in args],
        out_specs=tuple(pl.BlockSpec(o.shape, lambda i: (0, 0)) for o in outs),
        out_shape=outs,
        compiler_params=pltpu.CompilerParams(
            dimension_semantics=("arbitrary",), vmem_limit_bytes=VMEM_LIMIT_BYTES),
        name="sample_out",
    )(*args)


def _block_diag(wh):
    h, n, _ = wh.shape
    eye = jnp.eye(h, dtype=wh.dtype)
    return jnp.einsum("hij,hg->higj", wh, eye).reshape(h * n, h * n)


def kernel(x_prompt, x_sample, cache_k, cache_v, page_table, state_lru_h, state_conv, norm_pre, norm_post, w_in_ab, conv_w, conv_b, lru_wa, lru_ba, lru_wx, lru_bx, lru_lambda, w_out_ab, w_in_c, c_ln_g, c_ln_b, c_ws, c_bs, w_out_c):
    b, s, d = x_prompt.shape
    db, t_len, _ = x_sample.shape
    n_pages = page_table.shape[1]
    assert norm_pre.shape[0] == 2 and w_in_ab.shape[0] == 1 and w_in_c.shape[0] == 1
    assert s % (2 * MOBA_BLOCK) == 0 and t_len == V7X_SUBLANES
    assert (n_pages * PAGE_SIZE) % MOBA_BLOCK == 0 and t_len <= CHUNK
    assert cache_k.shape[2:] == (PAGE_SIZE, ATT_HEADS, HEAD_DIM)
    w = LRU_WIDTH
    row = lambda vec: vec.reshape(1, -1)

    whi = w_in_ab[0].astype(BF16)
    wq = w_in_ab[0][:, :2 * ATT_WIDTH]
    wlo = (wq - wq.astype(BF16).astype(F32)).astype(BF16)
    wa = _block_diag(lru_wa[0]).astype(BF16)
    wx = _block_diag(lru_wx[0]).astype(BF16)
    lru_args = (conv_w[0], row(conv_b[0]), wa, wx, row(lru_ba[0]), row(lru_bx[0]), row(lru_lambda[0]))
    wo = w_out_ab[0].astype(BF16)
    wi = w_in_c[0].astype(BF16)
    wc = w_out_c[0].astype(BF16)
    np0, np1 = row(norm_post[0]), row(norm_post[1])
    npre0, npre1 = row(norm_pre[0]), row(norm_pre[1])
    lg, lb = row(c_ln_g[0]), row(c_ln_b[0])
    bst = jnp.repeat(c_bs[0].T, GMLP_GROUP_DIM, axis=1)

    (qt, kt, vt, vtb, kn, km, sg, ml, h_last, conv_last) = _prompt_in(
        x_prompt, npre0, whi, wlo, *lru_args, tm=2 * MOBA_BLOCK)
    att = _prompt_attn(qt, kn, vtb, km)
    y_prompt = _prompt_out(
        x_prompt.reshape(b * s, d), att.reshape(b * s, ATT_WIDTH), sg.reshape(b * s, ATT_WIDTH),
        ml.reshape(b * s, w), np0, wo, npre1, wi, lg, lb, c_ws[0], bst, wc, np1,
        tm=MOBA_BLOCK).reshape(b, s, d)
    heads_last = lambda xt: xt.reshape(b, ATT_HEADS, HEAD_DIM, s).transpose(0, 3, 1, 2)[None]
    k_prompt, v_prompt = heads_last(kt), heads_last(vt)

    n = db * t_len
    xs = x_sample.reshape(n, d)
    h0rep = jnp.repeat(state_lru_h[0], t_len, axis=0)
    stpad = jnp.pad(state_conv[0], ((0, 0), (t_len - (CONV_WIDTH - 1), 0), (0, 0))).reshape(n, w)
    q_s, k_s, v_s, sg_s, ml_s, hs_s, xl_s = _sample_in(xs, npre0, whi, wlo, *lru_args, h0rep, stpad)

    cache_kt = cache_k.transpose(0, 1, 3, 4, 2)
    cache_vt = cache_v.transpose(0, 1, 3, 4, 2)
    idx = _sample_select(page_table, q_s, cache_kt, 0, pages_per_step=16)
    tok_last = lambda a: a.reshape(db, t_len, ATT_WIDTH).transpose(0, 2, 1)
    att_t = _sample_attn(page_table, idx[..., :MOBA_TOPK], tok_last(q_s), tok_last(k_s),
                         tok_last(v_s), cache_kt, cache_vt, 0)
    att_s = att_t.transpose(0, 2, 1).reshape(n, ATT_WIDTH)

    ws8 = c_ws[0][:, :t_len, :t_len]
    tt = jnp.arange(t_len)
    diag = lambda dd: jnp.where(tt >= dd, ws8[:, tt, jnp.maximum(tt - dd, 0)], 0.0)
    cd = jnp.stack([jnp.repeat(diag(dd).T, GMLP_GROUP_DIM, axis=1) for dd in range(t_len)])
    y_s, gv_s = _sample_out(xs, att_s, sg_s, ml_s, np0, wo, npre1, wi, lg, lb, cd, bst[:t_len],
                            wc, np1)

    per_req = lambda a, width: a.reshape(db, t_len, width)
    return (y_prompt, y_s.reshape(db, t_len, d), k_prompt, v_prompt,
            h_last.reshape(1, b, w), conv_last[None],
            k_s.reshape(1, db, t_len, ATT_HEADS, HEAD_DIM), v_s.reshape(1, db, t_len, ATT_HEADS, HEAD_DIM),
            per_req(hs_s, w)[:, t_len - 1][None], per_req(xl_s, w)[:, t_len - (CONV_WIDTH - 1):][None],
            per_req(gv_s, GMLP_WIDTH)[None])
```

```python
import functools

import jax
import jax.numpy as jnp
from jax import lax
from jax.experimental import pallas as pl
from jax.experimental.pallas import tpu as pltpu

ATT_HEADS = 8
HEAD_DIM = 64
ATT_WIDTH = ATT_HEADS * HEAD_DIM
HEAD_PAIRS = ATT_HEADS // 2
MOBA_BLOCK = 256
MOBA_TOPK = 3
QUERY_BLOCK = 128
LRU_WIDTH = 512
CONV_WIDTH = 4
LRU_C = 8.0
GMLP_WIDTH = 1024
GMLP_GROUPS = 8
GMLP_GROUP_DIM = GMLP_WIDTH // GMLP_GROUPS
CHUNK = 128
PAGE_SIZE = 128
NORM_EPS = 1e-6
ATT_SCALE = HEAD_DIM ** -0.5
LOG2_E = 1.4426950408889634

V7X_LANES = 128
V7X_SUBLANES = 8
VMEM_LIMIT_BYTES = 56 * 1024 * 1024

F32 = jnp.float32
BF16 = jnp.bfloat16
NEG_INF = float("-inf")
POS_INF = float("inf")


def _rms_norm(x, g):
    return x * lax.rsqrt(jnp.mean(x * x, axis=-1, keepdims=True) + NORM_EPS) * g


def _layer_norm(x, g, b):
    mu = jnp.mean(x, axis=-1, keepdims=True)
    xc = x - mu
    var = jnp.mean(xc * xc, axis=-1, keepdims=True)
    return xc * lax.rsqrt(var + NORM_EPS) * g + b


def _sigmoid(x):
    return 1.0 / (1.0 + jnp.exp(-x))


def _silu(x):
    return x * _sigmoid(x)


def _gelu_tanh(x):
    c = 0.7978845608028654
    return 0.5 * x * (1.0 + jnp.tanh(c * (x + 0.044715 * (x * x * x))))


def _softplus(x):
    return jnp.maximum(x, 0.0) + jnp.log1p(jnp.exp(-jnp.abs(x)))


def _dot(a, b):
    return jnp.dot(a, b, preferred_element_type=F32)


def _dot_exact(a, b):
    return jnp.dot(a, b, preferred_element_type=F32, precision=lax.Precision.HIGHEST)


def _split_bf16(x):
    hi = x.astype(BF16)
    lo = (x - hi.astype(F32)).astype(BF16)
    return hi, lo


def _in_proj_ab(xn, whi_ref, wlo_ref):
    qk_w = 2 * ATT_WIDTH
    xh, xl = _split_bf16(xn)
    qk = (_dot(xh, whi_ref[:, :qk_w]) + _dot(xh, wlo_ref[...])
          + _dot(xl, whi_ref[:, :qk_w]))
    rest = _dot(xh, whi_ref[:, qk_w:])
    return qk, rest


def _lru_coeffs(xc, wa_ref, wx_ref, ba_ref, bx_ref, lam_ref):
    xcb = xc.astype(BF16)
    r = _sigmoid(_dot(xcb, wa_ref[...]) + ba_ref[...])
    i = _sigmoid(_dot(xcb, wx_ref[...]) + bx_ref[...])
    log_a = (-LRU_C) * r * _softplus(-lam_ref[...])
    a = jnp.exp(log_a)
    u = jnp.sqrt(-jnp.tanh(log_a) * (a * a + 1.0)) * (i * xc)
    return a, u


def _top3_rows(gate, blk, n_valid):
    nb = gate.shape[0]
    sel = jnp.zeros(gate.shape, F32)
    for r in range(MOBA_TOPK):
        mx = jnp.max(gate, axis=0, keepdims=True)
        ix = jnp.min(jnp.where(gate == mx, blk, nb), axis=0, keepdims=True)
        hit = blk == ix
        sel = jnp.maximum(sel, jnp.where(hit, (r < n_valid).astype(F32), 0.0))
        gate = jnp.where(hit, NEG_INF, gate)
    return sel


def _prompt_in_kernel(x_ref, g_ref, whi_ref, wlo_ref, cw_ref, cb_ref, wa_ref, wx_ref,
                      ba_ref, bx_ref, lam_ref,
                      qt_ref, kt_ref, vt_ref, vtb_ref, kn_ref, km_ref, sg_ref, ml_ref,
                      hl_ref, cl_ref,
                      xbuf, hcar, abuf, ubuf):
    t = pl.program_id(1)
    tm = x_ref.shape[0]
    w = LRU_WIDTH

    @pl.when(t == 0)
    def _():
        xbuf[0:V7X_SUBLANES, :] = jnp.zeros((V7X_SUBLANES, w), F32)
        hcar[...] = jnp.zeros(hcar.shape, F32)

    xn = _rms_norm(x_ref[...], g_ref[...])
    qk, rest = _in_proj_ab(xn, whi_ref, wlo_ref)
    q = qk[:, :ATT_WIDTH]
    k = qk[:, ATT_WIDTH:]
    v = rest[:, :ATT_WIDTH]
    g_att = rest[:, ATT_WIDTH:2 * ATT_WIDTH]
    x_lru = rest[:, 2 * ATT_WIDTH:2 * ATT_WIDTH + w]
    g_lru = rest[:, 2 * ATT_WIDTH + w:]

    for s in range(tm // QUERY_BLOCK):
        rows = slice(s * QUERY_BLOCK, (s + 1) * QUERY_BLOCK)
        qt_ref[s] = q[rows, :].T
    kt_ref[...] = k.T
    vt = v.T
    vt_ref[...] = vt
    kn_ref[...] = k.astype(BF16)
    for s in range(tm // MOBA_BLOCK):
        vtb_ref[s] = vt[:, s * MOBA_BLOCK:(s + 1) * MOBA_BLOCK].astype(BF16)
        km_ref[pl.ds(t * (tm // MOBA_BLOCK) + s, 1), :] = jnp.mean(
            k[s * MOBA_BLOCK:(s + 1) * MOBA_BLOCK, :], axis=0, keepdims=True)
    sg_ref[...] = _silu(g_att).astype(BF16)

    xbuf[V7X_SUBLANES:, :] = x_lru
    xc = cw_ref[CONV_WIDTH - 1:CONV_WIDTH, :] * x_lru + cb_ref[...]
    for back in range(1, CONV_WIDTH):
        xc = xc + (cw_ref[CONV_WIDTH - 1 - back:CONV_WIDTH - back, :]
                   * xbuf[pl.ds(V7X_SUBLANES - back, tm), :])
    xbuf[0:V7X_SUBLANES, :] = x_lru[tm - V7X_SUBLANES:, :]

    a, u = _lru_coeffs(xc, wa_ref, wx_ref, ba_ref, bx_ref, lam_ref)
    abuf[...] = a
    ubuf[...] = u

    row = lax.broadcasted_iota(jnp.int32, (V7X_SUBLANES, w), 0)

    def group(gi, h):
        r0 = pl.multiple_of(gi * V7X_SUBLANES, V7X_SUBLANES)
        ag = abuf[pl.ds(r0, V7X_SUBLANES), :]
        ug = ubuf[pl.ds(r0, V7X_SUBLANES), :]
        for d in (1, 2, 4):
            keep = row >= d
            ug = jnp.where(keep, ag * pltpu.roll(ug, d, 0) + ug, ug)
            ag = jnp.where(keep, ag * pltpu.roll(ag, d, 0), ag)
        hg = ag * h + ug
        ubuf[pl.ds(r0, V7X_SUBLANES), :] = hg
        return jnp.broadcast_to(hg[V7X_SUBLANES - 1:, :], (V7X_SUBLANES, w))

    hcar[...] = lax.fori_loop(0, tm // V7X_SUBLANES, group, hcar[...])
    hs = ubuf[...]
    ml_ref[...] = (_silu(g_lru) * hs).astype(BF16)

    @pl.when(t == pl.num_programs(1) - 1)
    def _():
        hl_ref[...] = hs[tm - 1:, :]
        cl_ref[...] = x_lru[tm - (CONV_WIDTH - 1):, :]


def _prompt_in(x, g_pre, whi, wlo, cw, cb, wa, wx, ba, bx, lam, tm):
    b, s, d = x.shape
    w = LRU_WIDTH
    nb = s // MOBA_BLOCK
    nq = s // QUERY_BLOCK
    const = lambda shape: pl.BlockSpec(shape, lambda bi, ti: (0,) * len(shape))
    out_shape = (
        jax.ShapeDtypeStruct((b, nq, ATT_WIDTH, QUERY_BLOCK), F32),
        jax.ShapeDtypeStruct((b, ATT_WIDTH, s), F32),
        jax.ShapeDtypeStruct((b, ATT_WIDTH, s), F32),
        jax.ShapeDtypeStruct((b, nb, ATT_WIDTH, MOBA_BLOCK), BF16),
        jax.ShapeDtypeStruct((b, s, ATT_WIDTH), BF16),
        jax.ShapeDtypeStruct((b, nb, ATT_WIDTH), F32),
        jax.ShapeDtypeStruct((b, s, ATT_WIDTH), BF16),
        jax.ShapeDtypeStruct((b, s, w), BF16),
        jax.ShapeDtypeStruct((b, 1, w), F32),
        jax.ShapeDtypeStruct((b, CONV_WIDTH - 1, w), F32),
    )
    out_specs = (
        pl.BlockSpec((None, tm // QUERY_BLOCK, ATT_WIDTH, QUERY_BLOCK), lambda bi, ti: (bi, ti, 0, 0)),
        pl.BlockSpec((None, ATT_WIDTH, tm), lambda bi, ti: (bi, 0, ti)),
        pl.BlockSpec((None, ATT_WIDTH, tm), lambda bi, ti: (bi, 0, ti)),
        pl.BlockSpec((None, tm // MOBA_BLOCK, ATT_WIDTH, MOBA_BLOCK), lambda bi, ti: (bi, ti, 0, 0)),
        pl.BlockSpec((None, tm, ATT_WIDTH), lambda bi, ti: (bi, ti, 0)),
        pl.BlockSpec((None, nb, ATT_WIDTH), lambda bi, ti: (bi, 0, 0)),
        pl.BlockSpec((None, tm, ATT_WIDTH), lambda bi, ti: (bi, ti, 0)),
        pl.BlockSpec((None, tm, w), lambda bi, ti: (bi, ti, 0)),
        pl.BlockSpec((None, 1, w), lambda bi, ti: (bi, 0, 0)),
        pl.BlockSpec((None, CONV_WIDTH - 1, w), lambda bi, ti: (bi, 0, 0)),
    )
    in_specs = [
        pl.BlockSpec((None, tm, d), lambda bi, ti: (bi, ti, 0)),
        const((1, d)), const(whi.shape), const(wlo.shape), const(cw.shape), const(cb.shape),
        const(wa.shape), const(wx.shape), const(ba.shape), const(bx.shape), const(lam.shape),
    ]
    return pl.pallas_call(
        _prompt_in_kernel,
        grid=(b, s // tm),
        in_specs=in_specs,
        out_specs=out_specs,
        out_shape=out_shape,
        scratch_shapes=[
            pltpu.VMEM((tm + V7X_SUBLANES, w), F32),
            pltpu.VMEM((V7X_SUBLANES, w), F32),
            pltpu.VMEM((tm, w), F32),
            pltpu.VMEM((tm, w), F32),
        ],
        compiler_params=pltpu.CompilerParams(
            dimension_semantics=("arbitrary", "arbitrary"),
            vmem_limit_bytes=VMEM_LIMIT_BYTES),
        name="prompt_in",
    )(x, g_pre, whi, wlo, cw, cb, wa, wx, ba, bx, lam)


def _prompt_attn_kernel(qt_ref, kn_ref, vtb_ref, km_ref, o_ref, w_ref, sel_ref, sa_ref, sb_ref,
                        acc_ref):
    nsb = vtb_ref.shape[0]
    nb = km_ref.shape[0]
    tq = MOBA_BLOCK
    hd = HEAD_DIM
    ones_rows = 2 * V7X_SUBLANES
    first = lax.broadcasted_iota(jnp.int32, (2 * hd, tq), 0) < hd
    kmp = km_ref[...]
    blk = lax.broadcasted_iota(jnp.int32, (nb, tq), 0)
    ones_tile = jnp.ones((ones_rows, tq), BF16)
    causal = (lax.broadcasted_iota(jnp.int32, (tq, tq), 0)
              <= lax.broadcasted_iota(jnp.int32, (tq, tq), 1))

    def stage(sb, c):
        qt = jnp.concatenate([qt_ref[2 * sb], qt_ref[2 * sb + 1]], axis=1)
        for h in range(2):
            qth = jnp.where(first, qt, 0.0) if h == 0 else jnp.where(first, 0.0, qt)
            gate = _dot_exact(kmp, qth)
            gate = jnp.where(blk < sb, gate, NEG_INF)
            sel_ref[sb, h] = _top3_rows(gate, blk, sb)
            w_ref[sb, :, h * tq:(h + 1) * tq] = (qth * (ATT_SCALE * LOG2_E)).astype(BF16)
        return c

    lax.fori_loop(0, nsb, stage, 0, unroll=2)

    def v_lhs(j, h):
        return jnp.concatenate([vtb_ref[j, h * hd:(h + 1) * hd, :], ones_tile], axis=0)

    def key_block(j):
        return kn_ref[pl.ds(pl.multiple_of(j * MOBA_BLOCK, MOBA_BLOCK), MOBA_BLOCK), :]

    heads = (0, 1)
    bufs = (sa_ref, sb_ref)
    hcols = tuple(slice(h * tq, (h + 1) * tq) for h in heads)

    def scores_into(buf, j, wq):
        sc = _dot(key_block(j), wq)
        buf[...] = sc
        return tuple(jnp.max(sc[:, hcols[h]], axis=0, keepdims=True) for h in heads)

    def super_block(sb, c):
        last = sb - 1
        wq = w_ref[sb]
        bufs[0][...] = _dot(key_block(sb), wq)
        cm_first = scores_into(bufs[1], 0, wq)
        m_init = []
        for h in heads:
            s = jnp.where(causal, bufs[0][:, hcols[h]], NEG_INF)
            m = jnp.max(s, axis=0, keepdims=True)
            acc_ref[h] = _dot(v_lhs(sb, h), jnp.exp2(s - m).astype(BF16))
            m_init.append(m)

        def past_block(j, live, cur, nxt, carry):
            m_old, cm = carry
            cm_next = scores_into(bufs[nxt], jnp.minimum(j + 1, last), wq)
            m_out = []
            for h in heads:
                on = sel_ref[sb, h, pl.ds(j, 1), :] * live > 0.0
                m_new = jnp.maximum(m_old[h], jnp.where(on, cm[h], NEG_INF))
                p = jnp.exp2(bufs[cur][:, hcols[h]] - jnp.where(on, m_new, POS_INF)).astype(BF16)
                acc_ref[h] = acc_ref[h] * jnp.exp2(m_old[h] - m_new) + _dot(v_lhs(j, h), p)
                m_out.append(m_new)
            return tuple(m_out), cm_next

        def past_pair(j0, carry):
            j1 = jnp.minimum(j0 + 1, last)
            mid = past_block(j0, 1.0, 1, 0, carry)
            return past_block(j1, (j0 + 1 < sb).astype(F32), 0, 1, mid)

        n_quads = sb // 4
        carry = lax.fori_loop(
            0, n_quads, lambda i, cr: past_pair(4 * i + 2, past_pair(4 * i, cr)),
            (tuple(m_init), cm_first))
        lax.fori_loop(0, (sb - 4 * n_quads + 1) // 2,
                      lambda i, cr: past_pair(4 * n_quads + 2 * i, cr), carry)
        outs = []
        for h in heads:
            a = acc_ref[h]
            outs.append(a[:hd, :] / a[hd:hd + 1, :])
        o_ref[pl.ds(pl.multiple_of(sb * tq, tq), tq), :] = jnp.concatenate(outs, axis=0).T
        return c

    lax.fori_loop(0, nsb, super_block, 0)


def _prompt_attn(qt, kn, vtb, km):
    b, nq, _, qb = qt.shape
    s = kn.shape[1]
    nb = km.shape[1]
    pw = 2 * HEAD_DIM
    tq = MOBA_BLOCK
    return pl.pallas_call(
        _prompt_attn_kernel,
        grid=(b, HEAD_PAIRS),
        in_specs=[
            pl.BlockSpec((None, nq, pw, qb), lambda bi, pi: (bi, 0, pi, 0)),
            pl.BlockSpec((None, s, pw), lambda bi, pi: (bi, 0, pi)),
            pl.BlockSpec((None, nb, pw, tq), lambda bi, pi: (bi, 0, pi, 0)),
            pl.BlockSpec((None, nb, pw), lambda bi, pi: (bi, 0, pi)),
        ],
        out_specs=pl.BlockSpec((None, s, pw), lambda bi, pi: (bi, 0, pi)),
        out_shape=jax.ShapeDtypeStruct((b, s, ATT_WIDTH), F32),
        scratch_shapes=[
            pltpu.VMEM((nb, pw, 2 * tq), BF16),
            pltpu.VMEM((nb, 2, nb, tq), F32),
            pltpu.VMEM((tq, 2 * tq), F32), pltpu.VMEM((tq, 2 * tq), F32),
            pltpu.VMEM((2, HEAD_DIM + 2 * V7X_SUBLANES, tq), F32),
        ],
        compiler_params=pltpu.CompilerParams(
            dimension_semantics=("arbitrary", "arbitrary"),
            vmem_limit_bytes=VMEM_LIMIT_BYTES),
        name="prompt_attn",
    )(qt, kn, vtb, km)


def _mix_out_ab(att, sg, ml, wo_ref):
    ma = (sg.astype(F32) * att).astype(BF16)
    return _dot(ma, wo_ref[:ATT_WIDTH, :]) + _dot(ml.astype(BF16), wo_ref[ATT_WIDTH:, :])


def _gmlp_in(y0, npre_ref, wi_ref, lg_ref, lb_ref):
    xn = _rms_norm(y0, npre_ref[...])
    pr = _dot(xn.astype(BF16), wi_ref[...])
    u = _gelu_tanh(pr[:, :GMLP_WIDTH])
    v = _layer_norm(_gelu_tanh(pr[:, GMLP_WIDTH:2 * GMLP_WIDTH]), lg_ref[...], lb_ref[...])
    g = pr[:, 2 * GMLP_WIDTH:]
    return u, v, g


def _prompt_out_kernel(x_ref, att_ref, sg_ref, ml_ref, np0_ref, wo_ref, npre_ref, wi_ref,
                       lg_ref, lb_ref, ws_ref, bst_ref, wc_ref, np1_ref, y_ref, mix_ref):
    tm = x_ref.shape[0]
    op = _mix_out_ab(att_ref[...], sg_ref[...], ml_ref[...], wo_ref)
    y0 = x_ref[...] + _rms_norm(op, np0_ref[...])
    u, v, g = _gmlp_in(y0, npre_ref, wi_ref, lg_ref, lb_ref)
    vb = v.astype(BF16)
    t_out = lax.broadcasted_iota(jnp.int32, (CHUNK, CHUNK), 0)
    t_in = lax.broadcasted_iota(jnp.int32, (CHUNK, CHUNK), 1)
    for gi in range(GMLP_GROUPS):
        wm = jnp.where(t_in <= t_out, ws_ref[gi], 0.0).astype(BF16)
        cols = slice(gi * GMLP_GROUP_DIM, (gi + 1) * GMLP_GROUP_DIM)
        for c in range(tm // CHUNK):
            rows = slice(c * CHUNK, (c + 1) * CHUNK)
            mix_ref[rows, cols] = _dot(wm, vb[rows, cols]) + bst_ref[:, cols]
    z = _silu(g) * (u * mix_ref[...])
    op1 = _dot(z.astype(BF16), wc_ref[...])
    y_ref[...] = y0 + _rms_norm(op1, np1_ref[...])


def _prompt_out(x2, att2, sg2, ml2, np0, wo, npre1, wi, lg, lb, ws, bst, wc, np1, tm):
    n, d = x2.shape
    const = lambda shape: pl.BlockSpec(shape, lambda i: (0,) * len(shape))
    rows = lambda width: pl.BlockSpec((tm, width), lambda i: (i, 0))
    return pl.pallas_call(
        _prompt_out_kernel,
        grid=(n // tm,),
        in_specs=[rows(d), rows(ATT_WIDTH), rows(ATT_WIDTH), rows(LRU_WIDTH),
                  const(np0.shape), const(wo.shape), const(npre1.shape), const(wi.shape),
                  const(lg.shape), const(lb.shape), const(ws.shape), const(bst.shape),
                  const(wc.shape), const(np1.shape)],
        out_specs=rows(d),
        out_shape=jax.ShapeDtypeStruct((n, d), F32),
        scratch_shapes=[pltpu.VMEM((tm, GMLP_WIDTH), F32)],
        compiler_params=pltpu.CompilerParams(
            dimension_semantics=("arbitrary",),
            vmem_limit_bytes=VMEM_LIMIT_BYTES),
        name="prompt_out",
    )(x2, att2, sg2, ml2, np0, wo, npre1, wi, lg, lb, ws, bst, wc, np1)


def _sample_in_kernel(x_ref, g_ref, whi_ref, wlo_ref, cw_ref, cb_ref, wa_ref, wx_ref,
                      ba_ref, bx_ref, lam_ref, h0_ref, st_ref,
                      q_ref, k_ref, v_ref, sg_ref, ml_ref, hs_ref, xl_ref):
    n = x_ref.shape[0]
    t_len = V7X_SUBLANES
    w = LRU_WIDTH
    xn = _rms_norm(x_ref[...], g_ref[...])
    qk, rest = _in_proj_ab(xn, whi_ref, wlo_ref)
    q_ref[...] = qk[:, :ATT_WIDTH]
    k_ref[...] = qk[:, ATT_WIDTH:]
    v_ref[...] = rest[:, :ATT_WIDTH]
    g_att = rest[:, ATT_WIDTH:2 * ATT_WIDTH]
    x_lru = rest[:, 2 * ATT_WIDTH:2 * ATT_WIDTH + w]
    g_lru = rest[:, 2 * ATT_WIDTH + w:]
    sg_ref[...] = _silu(g_att)
    xl_ref[...] = x_lru

    tok = lax.broadcasted_iota(jnp.int32, (n, w), 0) % t_len
    st = st_ref[...]
    xc = cw_ref[CONV_WIDTH - 1:CONV_WIDTH, :] * x_lru + cb_ref[...]
    for back in range(1, CONV_WIDTH):
        prev = jnp.where(tok >= back, pltpu.roll(x_lru, back, 0),
                         pltpu.roll(st, n - t_len + back, 0))
        xc = xc + cw_ref[CONV_WIDTH - 1 - back:CONV_WIDTH - back, :] * prev

    a, u = _lru_coeffs(xc, wa_ref, wx_ref, ba_ref, bx_ref, lam_ref)
    for d in (1, 2, 4):
        keep = tok >= d
        u = jnp.where(keep, a * pltpu.roll(u, d, 0) + u, u)
        a = jnp.where(keep, a * pltpu.roll(a, d, 0), a)
    hs = a * h0_ref[...] + u
    hs_ref[...] = hs
    ml_ref[...] = _silu(g_lru) * hs


def _sample_in(xs, g_pre, whi, wlo, cw, cb, wa, wx, ba, bx, lam, h0rep, stpad):
    n, d = xs.shape
    w = LRU_WIDTH
    args = (xs, g_pre, whi, wlo, cw, cb, wa, wx, ba, bx, lam, h0rep, stpad)
    full = lambda a: pl.BlockSpec(a.shape, lambda i: (0,) * a.ndim)
    outs = [jax.ShapeDtypeStruct((n, ATT_WIDTH), F32)] * 4 + [jax.ShapeDtypeStruct((n, w), F32)] * 3
    return pl.pallas_call(
        _sample_in_kernel,
        grid=(1,),
        in_specs=[full(a) for a in args],
        out_specs=tuple(pl.BlockSpec(o.shape, lambda i: (0, 0)) for o in outs),
        out_shape=tuple(outs),
        compiler_params=pltpu.CompilerParams(
            dimension_semantics=("arbitrary",), vmem_limit_bytes=VMEM_LIMIT_BYTES),
        name="sample_in",
    )(*args)


def _sample_select_kernel(pt_ref, q_ref, ptv_ref, *refs, pages_per_step, n_blocks):
    k_refs = refs[:pages_per_step]
    idx_ref = refs[pages_per_step]
    kmt = refs[pages_per_step + 1]
    c = pl.program_id(1)
    bp = MOBA_BLOCK // PAGE_SIZE
    lane3 = lax.broadcasted_iota(jnp.int32, kmt.shape, 2)

    @pl.when(c == 0)
    def _():
        kmt[...] = jnp.zeros(kmt.shape, F32)

    for i in range(pages_per_step // bp):
        tot = k_refs[bp * i][...]
        for pg in range(1, bp):
            tot = tot + k_refs[bp * i + pg][...]
        col = jnp.sum(tot, axis=-1, keepdims=True) * (1.0 / MOBA_BLOCK)
        blk = c * (pages_per_step // bp) + i
        kmt[...] = jnp.where(lane3 == blk, col, kmt[...])

    @pl.when(c == pl.num_programs(1) - 1)
    def _():
        qv = q_ref[...]
        t_len = qv.shape[0]
        lane = lax.broadcasted_iota(jnp.int32, (t_len, V7X_LANES), 1)
        lane_pair = lax.broadcasted_iota(jnp.int32, (t_len, 2 * HEAD_DIM), 1)
        pages = jnp.broadcast_to(ptv_ref[...].astype(F32), (t_len, V7X_LANES))
        for p in range(HEAD_PAIRS):
            qp = qv[:, p * 2 * HEAD_DIM:(p + 1) * 2 * HEAD_DIM]
            kmp = kmt[2 * p:2 * p + 2].reshape(2 * HEAD_DIM, V7X_LANES)
            for hh in range(2):
                qm = jnp.where((lane_pair < HEAD_DIM) == (hh == 0), qp, 0.0)
                gate = _dot_exact(qm, kmp)
                gate = jnp.where(lane < n_blocks, gate, NEG_INF)
                out = jnp.zeros((t_len, V7X_LANES), jnp.int32)
                for r in range(MOBA_TOPK):
                    mx = jnp.max(gate, axis=-1, keepdims=True)
                    ix = jnp.min(jnp.where(gate == mx, lane, V7X_LANES), axis=-1, keepdims=True)
                    gate = jnp.where(lane == ix, NEG_INF, gate)
                    for pg in range(bp):
                        phys = jnp.sum(jnp.where(lane == ix * bp + pg, pages, 0.0),
                                       axis=-1, keepdims=True)
                        out = jnp.where(lane == r * bp + pg, phys.astype(jnp.int32), out)
                idx_ref[2 * p + hh] = out


def _sample_select(page_table, q_s, cache_t, layer, pages_per_step):
    db, n_pages = page_table.shape
    t_len = q_s.shape[0] // db
    n_blocks = n_pages * PAGE_SIZE // MOBA_BLOCK
    assert n_pages == V7X_LANES and n_pages % pages_per_step == 0

    def page_spec(i):
        return pl.BlockSpec(
            (None, None, ATT_HEADS, HEAD_DIM, PAGE_SIZE),
            lambda b, c, pt: (layer, pt[b * n_pages + c * pages_per_step + i], 0, 0, 0))

    grid_spec = pltpu.PrefetchScalarGridSpec(
        num_scalar_prefetch=1,
        grid=(db, n_pages // pages_per_step),
        in_specs=[pl.BlockSpec((t_len, ATT_WIDTH), lambda b, c, pt: (b, 0)),
                  pl.BlockSpec((None, 1, n_pages), lambda b, c, pt: (b, 0, 0))]
                 + [page_spec(i) for i in range(pages_per_step)],
        out_specs=pl.BlockSpec((None, ATT_HEADS, t_len, V7X_LANES), lambda b, c, pt: (b, 0, 0, 0)),
        scratch_shapes=[pltpu.VMEM((ATT_HEADS, HEAD_DIM, V7X_LANES), F32)],
    )
    return pl.pallas_call(
        functools.partial(_sample_select_kernel, pages_per_step=pages_per_step, n_blocks=n_blocks),
        grid_spec=grid_spec,
        out_shape=jax.ShapeDtypeStruct((db, ATT_HEADS, t_len, V7X_LANES), jnp.int32),
        compiler_params=pltpu.CompilerParams(
            dimension_semantics=("arbitrary", "arbitrary"), vmem_limit_bytes=VMEM_LIMIT_BYTES),
        name="sample_select",
    )(page_table.reshape(-1), q_s, page_table.reshape(db, 1, n_pages),
      *([cache_t] * pages_per_step))


def _sample_attn_kernel(ph_ref, qt_ref, kt_ref, vt_ref, *refs, t_len):
    n_sel = MOBA_TOPK * (MOBA_BLOCK // PAGE_SIZE)
    k_refs = refs[:t_len * n_sel]
    v_refs = refs[t_len * n_sel:2 * t_len * n_sel]
    o_ref = refs[2 * t_len * n_sel]
    ktn = kt_ref[...]
    vtn = vt_ref[...]
    tok = lax.broadcasted_iota(jnp.int32, (1, t_len), 1)
    for t in range(t_len):
        qc = qt_ref[:, t:t + 1] * ATT_SCALE
        s_own = jnp.sum(ktn * qc, axis=0, keepdims=True)
        s_own = jnp.where(tok <= t, s_own, NEG_INF)
        s_sel = [jnp.sum(k_refs[t * n_sel + i][...] * qc, axis=0, keepdims=True)
                 for i in range(n_sel)]
        m = jnp.max(s_own, axis=-1, keepdims=True)
        for s in s_sel:
            m = jnp.maximum(m, jnp.max(s, axis=-1, keepdims=True))
        p_own = jnp.exp(s_own - m)
        den = jnp.sum(p_own, axis=-1, keepdims=True)
        acc = None
        for i, s in enumerate(s_sel):
            p = jnp.exp(s - m)
            den = den + jnp.sum(p, axis=-1, keepdims=True)
            pv = v_refs[t * n_sel + i][...] * p
            acc = pv if acc is None else acc + pv
        o = jnp.sum(acc, axis=-1, keepdims=True) + jnp.sum(vtn * p_own, axis=-1, keepdims=True)
        o_ref[:, t:t + 1] = o / den


def _sample_attn(phys, qt_s, kt_s, vt_s, cache_kt, cache_vt, layer):
    db, _, t_len = qt_s.shape
    n_sel = MOBA_TOPK * (MOBA_BLOCK // PAGE_SIZE)
    per_bh = t_len * n_sel

    def page_spec(i):
        return pl.BlockSpec(
            (None, None, None, HEAD_DIM, PAGE_SIZE),
            lambda b, h, ph: (layer, ph[(b * ATT_HEADS + h) * per_bh + i], h, 0, 0))

    new_spec = pl.BlockSpec((None, HEAD_DIM, t_len), lambda b, h, ph: (b, h, 0))
    grid_spec = pltpu.PrefetchScalarGridSpec(
        num_scalar_prefetch=1,
        grid=(db, ATT_HEADS),
        in_specs=[new_spec, new_spec, new_spec] + [page_spec(i) for i in range(per_bh)] * 2,
        out_specs=new_spec,
    )
    return pl.pallas_call(
        functools.partial(_sample_attn_kernel, t_len=t_len),
        grid_spec=grid_spec,
        out_shape=jax.ShapeDtypeStruct((db, ATT_WIDTH, t_len), F32),
        compiler_params=pltpu.CompilerParams(
            dimension_semantics=("arbitrary", "arbitrary"), vmem_limit_bytes=VMEM_LIMIT_BYTES),
        name="sample_attn",
    )(phys.reshape(-1), qt_s, kt_s, vt_s, *([cache_kt] * per_bh), *([cache_vt] * per_bh))


def _sample_out_kernel(x_ref, att_ref, sg_ref, ml_ref, np0_ref, wo_ref, npre_ref, wi_ref,
                       lg_ref, lb_ref, cd_ref, bst_ref, wc_ref, np1_ref, y_ref, gv_ref):
    n = x_ref.shape[0]
    t_len = cd_ref.shape[1]
    reps = n // t_len
    op = _mix_out_ab(att_ref[...], sg_ref[...], ml_ref[...], wo_ref)
    y0 = x_ref[...] + _rms_norm(op, np0_ref[...])
    u, v, g = _gmlp_in(y0, npre_ref, wi_ref, lg_ref, lb_ref)
    gv_ref[...] = v
    tile = lambda tab: jnp.concatenate([tab] * reps, axis=0)
    mix = tile(bst_ref[...]) + tile(cd_ref[0]) * v
    for d in range(1, t_len):
        mix = mix + tile(cd_ref[d]) * pltpu.roll(v, d, 0)
    z = _silu(g) * (u * mix)
    op1 = _dot(z.astype(BF16), wc_ref[...])
    y_ref[...] = y0 + _rms_norm(op1, np1_ref[...])


def _sample_out(xs, att, sg, ml, np0, wo, npre1, wi, lg, lb, cd, bst8, wc, np1):
    n, d = xs.shape
    args = (xs, att, sg, ml, np0, wo, npre1, wi, lg, lb, cd, bst8, wc, np1)
    full = lambda a: pl.BlockSpec(a.shape, lambda i: (0,) * a.ndim)
    outs = (jax.ShapeDtypeStruct((n, d), F32), jax.ShapeDtypeStruct((n, GMLP_WIDTH), F32))
    return pl.pallas_call(
        _sample_out_kernel,
        grid=(1,),
        in_specs=[full(a) for a in args],
        out_specs=tuple(pl.BlockSpec(o.shape, lambda i: (0, 0)) for o in outs),
        out_shape=outs,
        compiler_params=pltpu.CompilerParams(
            dimension_semantics=("arbitrary",), vmem_limit_bytes=VMEM_LIMIT_BYTES),
        name="sample_out",
    )(*args)


def _block_diag(wh):
    h, n, _ = wh.shape
    eye = jnp.eye(h, dtype=wh.dtype)
    return jnp.einsum("hij,hg->higj", wh, eye).reshape(h * n, h * n)


def kernel(x_prompt, x_sample, cache_k, cache_v, page_table, state_lru_h, state_conv, norm_pre, norm_post, w_in_ab, conv_w, conv_b, lru_wa, lru_ba, lru_wx, lru_bx, lru_lambda, w_out_ab, w_in_c, c_ln_g, c_ln_b, c_ws, c_bs, w_out_c):
    b, s, d = x_prompt.shape
    db, t_len, _ = x_sample.shape
    n_pages = page_table.shape[1]
    assert norm_pre.shape[0] == 2 and w_in_ab.shape[0] == 1 and w_in_c.shape[0] == 1
    assert s % (2 * MOBA_BLOCK) == 0 and t_len == V7X_SUBLANES
    assert (n_pages * PAGE_SIZE) % MOBA_BLOCK == 0 and t_len <= CHUNK
    assert cache_k.shape[2:] == (PAGE_SIZE, ATT_HEADS, HEAD_DIM)
    w = LRU_WIDTH
    row = lambda vec: vec.reshape(1, -1)

    whi = w_in_ab[0].astype(BF16)
    wq = w_in_ab[0][:, :2 * ATT_WIDTH]
    wlo = (wq - wq.astype(BF16).astype(F32)).astype(BF16)
    wa = _block_diag(lru_wa[0]).astype(BF16)
    wx = _block_diag(lru_wx[0]).astype(BF16)
    lru_args = (conv_w[0], row(conv_b[0]), wa, wx, row(lru_ba[0]), row(lru_bx[0]), row(lru_lambda[0]))
    wo = w_out_ab[0].astype(BF16)
    wi = w_in_c[0].astype(BF16)
    wc = w_out_c[0].astype(BF16)
    np0, np1 = row(norm_post[0]), row(norm_post[1])
    npre0, npre1 = row(norm_pre[0]), row(norm_pre[1])
    lg, lb = row(c_ln_g[0]), row(c_ln_b[0])
    bst = jnp.repeat(c_bs[0].T, GMLP_GROUP_DIM, axis=1)

    (qt, kt, vt, vtb, kn, km, sg, ml, h_last, conv_last) = _prompt_in(
        x_prompt, npre0, whi, wlo, *lru_args, tm=2 * MOBA_BLOCK)
    att = _prompt_attn(qt, kn, vtb, km)
    y_prompt = _prompt_out(
        x_prompt.reshape(b * s, d), att.reshape(b * s, ATT_WIDTH), sg.reshape(b * s, ATT_WIDTH),
        ml.reshape(b * s, w), np0, wo, npre1, wi, lg, lb, c_ws[0], bst, wc, np1,
        tm=MOBA_BLOCK).reshape(b, s, d)
    heads_last = lambda xt: xt.reshape(b, ATT_HEADS, HEAD_DIM, s).transpose(0, 3, 1, 2)[None]
    k_prompt, v_prompt = heads_last(kt), heads_last(vt)

    n = db * t_len
    xs = x_sample.reshape(n, d)
    h0rep = jnp.repeat(state_lru_h[0], t_len, axis=0)
    stpad = jnp.pad(state_conv[0], ((0, 0), (t_len - (CONV_WIDTH - 1), 0), (0, 0))).reshape(n, w)
    q_s, k_s, v_s, sg_s, ml_s, hs_s, xl_s = _sample_in(xs, npre0, whi, wlo, *lru_args, h0rep, stpad)

    cache_kt = cache_k.transpose(0, 1, 3, 4, 2)
    cache_vt = cache_v.transpose(0, 1, 3, 4, 2)
    phys = _sample_select(page_table, q_s, cache_kt, 0, pages_per_step=16)
    tok_last = lambda a: a.reshape(db, t_len, ATT_WIDTH).transpose(0, 2, 1)
    att_t = _sample_attn(phys[..., :MOBA_TOPK * (MOBA_BLOCK // PAGE_SIZE)], tok_last(q_s),
                         tok_last(k_s), tok_last(v_s), cache_kt, cache_vt, 0)
    att_s = att_t.transpose(0, 2, 1).reshape(n, ATT_WIDTH)

    ws8 = c_ws[0][:, :t_len, :t_len]
    tt = jnp.arange(t_len)
    diag = lambda dd: jnp.where(tt >= dd, ws8[:, tt, jnp.maximum(tt - dd, 0)], 0.0)
    cd = jnp.stack([jnp.repeat(diag(dd).T, GMLP_GROUP_DIM, axis=1) for dd in range(t_len)])
    y_s, gv_s = _sample_out(xs, att_s, sg_s, ml_s, np0, wo, npre1, wi, lg, lb, cd, bst[:t_len],
                            wc, np1)

    per_req = lambda a, width: a.reshape(db, t_len, width)
    return (y_prompt, y_s.reshape(db, t_len, d), k_prompt, v_prompt,
            h_last.reshape(1, b, w), conv_last[None],
            k_s.reshape(1, db, t_len, ATT_HEADS, HEAD_DIM), v_s.reshape(1, db, t_len, ATT_HEADS, HEAD_DIM),
            per_req(hs_s, w)[:, t_len - 1][None], per_req(xl_s, w)[:, t_len - (CONV_WIDTH - 1):][None],
            per_req(gv_s, GMLP_WIDTH)[None])
```

```python
import functools

import jax
import jax.numpy as jnp
from jax import lax
from jax.experimental import pallas as pl
from jax.experimental.pallas import tpu as pltpu

ATT_HEADS = 8
HEAD_DIM = 64
ATT_WIDTH = ATT_HEADS * HEAD_DIM
HEAD_PAIRS = ATT_HEADS // 2
MOBA_BLOCK = 256
MOBA_TOPK = 3
QUERY_BLOCK = 128
LRU_WIDTH = 512
CONV_WIDTH = 4
LRU_C = 8.0
GMLP_WIDTH = 1024
GMLP_GROUPS = 8
GMLP_GROUP_DIM = GMLP_WIDTH // GMLP_GROUPS
CHUNK = 128
PAGE_SIZE = 128
NORM_EPS = 1e-6
ATT_SCALE = HEAD_DIM ** -0.5
LOG2_E = 1.4426950408889634

V7X_LANES = 128
V7X_SUBLANES = 8
VMEM_LIMIT_BYTES = 56 * 1024 * 1024

F32 = jnp.float32
BF16 = jnp.bfloat16
NEG_INF = float("-inf")
POS_INF = float("inf")


def _rms_norm(x, g):
    return x * lax.rsqrt(jnp.mean(x * x, axis=-1, keepdims=True) + NORM_EPS) * g


def _layer_norm(x, g, b):
    mu = jnp.mean(x, axis=-1, keepdims=True)
    xc = x - mu
    var = jnp.mean(xc * xc, axis=-1, keepdims=True)
    return xc * lax.rsqrt(var + NORM_EPS) * g + b


def _sigmoid(x):
    return 1.0 / (1.0 + jnp.exp(-x))


def _silu(x):
    return x * _sigmoid(x)


def _gelu_tanh(x):
    c = 0.7978845608028654
    return 0.5 * x * (1.0 + jnp.tanh(c * (x + 0.044715 * (x * x * x))))


def _softplus(x):
    return jnp.maximum(x, 0.0) + jnp.log1p(jnp.exp(-jnp.abs(x)))


def _dot(a, b):
    return jnp.dot(a, b, preferred_element_type=F32)


def _dot_exact(a, b):
    return jnp.dot(a, b, preferred_element_type=F32, precision=lax.Precision.HIGHEST)


def _split_bf16(x):
    hi = x.astype(BF16)
    lo = (x - hi.astype(F32)).astype(BF16)
    return hi, lo


def _in_proj_ab(xn, whi_ref, wlo_ref):
    qk_w = 2 * ATT_WIDTH
    xh, xl = _split_bf16(xn)
    qk = (_dot(xh, whi_ref[:, :qk_w]) + _dot(xh, wlo_ref[...])
          + _dot(xl, whi_ref[:, :qk_w]))
    rest = _dot(xh, whi_ref[:, qk_w:])
    return qk, rest


def _lru_coeffs(xc, wa_ref, wx_ref, ba_ref, bx_ref, lam_ref):
    xcb = xc.astype(BF16)
    r = _sigmoid(_dot(xcb, wa_ref[...]) + ba_ref[...])
    i = _sigmoid(_dot(xcb, wx_ref[...]) + bx_ref[...])
    log_a = (-LRU_C) * r * _softplus(-lam_ref[...])
    a = jnp.exp(log_a)
    u = jnp.sqrt(-jnp.tanh(log_a) * (a * a + 1.0)) * (i * xc)
    return a, u


def _top3_rows(gate, blk, n_valid):
    nb = gate.shape[0]
    sel = jnp.zeros(gate.shape, F32)
    for r in range(MOBA_TOPK):
        mx = jnp.max(gate, axis=0, keepdims=True)
        ix = jnp.min(jnp.where(gate == mx, blk, nb), axis=0, keepdims=True)
        hit = blk == ix
        sel = jnp.maximum(sel, jnp.where(hit, jnp.where(r < n_valid, 1.0, 0.0), 0.0))
        gate = jnp.where(hit, NEG_INF, gate)
    return sel


def _prompt_in_kernel(x_ref, g_ref, whi_ref, wlo_ref, cw_ref, cb_ref, wa_ref, wx_ref,
                      ba_ref, bx_ref, lam_ref,
                      qt_ref, kt_ref, vt_ref, vtb_ref, kn_ref, km_ref, sg_ref, ml_ref,
                      hl_ref, cl_ref,
                      xbuf, hcar, abuf, ubuf):
    t = pl.program_id(1)
    tm = x_ref.shape[0]
    w = LRU_WIDTH

    @pl.when(t == 0)
    def _():
        xbuf[0:V7X_SUBLANES, :] = jnp.zeros((V7X_SUBLANES, w), F32)
        hcar[...] = jnp.zeros(hcar.shape, F32)

    xn = _rms_norm(x_ref[...], g_ref[...])
    qk, rest = _in_proj_ab(xn, whi_ref, wlo_ref)
    q = qk[:, :ATT_WIDTH]
    k = qk[:, ATT_WIDTH:]
    v = rest[:, :ATT_WIDTH]
    g_att = rest[:, ATT_WIDTH:2 * ATT_WIDTH]
    x_lru = rest[:, 2 * ATT_WIDTH:2 * ATT_WIDTH + w]
    g_lru = rest[:, 2 * ATT_WIDTH + w:]

    for s in range(tm // QUERY_BLOCK):
        rows = slice(s * QUERY_BLOCK, (s + 1) * QUERY_BLOCK)
        qt_ref[s] = q[rows, :].T
    kt_ref[...] = k.T
    vt = v.T
    vt_ref[...] = vt
    kn_ref[...] = k.astype(BF16)
    for s in range(tm // MOBA_BLOCK):
        vtb_ref[s] = vt[:, s * MOBA_BLOCK:(s + 1) * MOBA_BLOCK].astype(BF16)
        km_ref[pl.ds(t * (tm // MOBA_BLOCK) + s, 1), :] = jnp.mean(
            k[s * MOBA_BLOCK:(s + 1) * MOBA_BLOCK, :], axis=0, keepdims=True)
    sg_ref[...] = _silu(g_att).astype(BF16)

    xbuf[V7X_SUBLANES:, :] = x_lru
    xc = cw_ref[CONV_WIDTH - 1:CONV_WIDTH, :] * x_lru + cb_ref[...]
    for back in range(1, CONV_WIDTH):
        xc = xc + (cw_ref[CONV_WIDTH - 1 - back:CONV_WIDTH - back, :]
                   * xbuf[pl.ds(V7X_SUBLANES - back, tm), :])
    xbuf[0:V7X_SUBLANES, :] = x_lru[tm - V7X_SUBLANES:, :]

    a, u = _lru_coeffs(xc, wa_ref, wx_ref, ba_ref, bx_ref, lam_ref)
    abuf[...] = a
    ubuf[...] = u

    row = lax.broadcasted_iota(jnp.int32, (V7X_SUBLANES, w), 0)

    def group(gi, h):
        r0 = pl.multiple_of(gi * V7X_SUBLANES, V7X_SUBLANES)
        ag = abuf[pl.ds(r0, V7X_SUBLANES), :]
        ug = ubuf[pl.ds(r0, V7X_SUBLANES), :]
        for d in (1, 2, 4):
            keep = row >= d
            ug = jnp.where(keep, ag * pltpu.roll(ug, d, 0) + ug, ug)
            ag = jnp.where(keep, ag * pltpu.roll(ag, d, 0), ag)
        hg = ag * h + ug
        ubuf[pl.ds(r0, V7X_SUBLANES), :] = hg
        return jnp.broadcast_to(hg[V7X_SUBLANES - 1:, :], (V7X_SUBLANES, w))

    hcar[...] = lax.fori_loop(0, tm // V7X_SUBLANES, group, hcar[...])
    hs = ubuf[...]
    ml_ref[...] = (_silu(g_lru) * hs).astype(BF16)

    @pl.when(t == pl.num_programs(1) - 1)
    def _():
        hl_ref[...] = hs[tm - 1:, :]
        cl_ref[...] = x_lru[tm - (CONV_WIDTH - 1):, :]


def _prompt_in(x, g_pre, whi, wlo, cw, cb, wa, wx, ba, bx, lam, tm):
    b, s, d = x.shape
    w = LRU_WIDTH
    nb = s // MOBA_BLOCK
    nq = s // QUERY_BLOCK
    const = lambda shape: pl.BlockSpec(shape, lambda bi, ti: (0,) * len(shape))
    out_shape = (
        jax.ShapeDtypeStruct((b, nq, ATT_WIDTH, QUERY_BLOCK), F32),
        jax.ShapeDtypeStruct((b, ATT_WIDTH, s), F32),
        jax.ShapeDtypeStruct((b, ATT_WIDTH, s), F32),
        jax.ShapeDtypeStruct((b, nb, ATT_WIDTH, MOBA_BLOCK), BF16),
        jax.ShapeDtypeStruct((b, s, ATT_WIDTH), BF16),
        jax.ShapeDtypeStruct((b, nb, ATT_WIDTH), F32),
        jax.ShapeDtypeStruct((b, s, ATT_WIDTH), BF16),
        jax.ShapeDtypeStruct((b, s, w), BF16),
        jax.ShapeDtypeStruct((b, 1, w), F32),
        jax.ShapeDtypeStruct((b, CONV_WIDTH - 1, w), F32),
    )
    out_specs = (
        pl.BlockSpec((None, tm // QUERY_BLOCK, ATT_WIDTH, QUERY_BLOCK), lambda bi, ti: (bi, ti, 0, 0)),
        pl.BlockSpec((None, ATT_WIDTH, tm), lambda bi, ti: (bi, 0, ti)),
        pl.BlockSpec((None, ATT_WIDTH, tm), lambda bi, ti: (bi, 0, ti)),
        pl.BlockSpec((None, tm // MOBA_BLOCK, ATT_WIDTH, MOBA_BLOCK), lambda bi, ti: (bi, ti, 0, 0)),
        pl.BlockSpec((None, tm, ATT_WIDTH), lambda bi, ti: (bi, ti, 0)),
        pl.BlockSpec((None, nb, ATT_WIDTH), lambda bi, ti: (bi, 0, 0)),
        pl.BlockSpec((None, tm, ATT_WIDTH), lambda bi, ti: (bi, ti, 0)),
        pl.BlockSpec((None, tm, w), lambda bi, ti: (bi, ti, 0)),
        pl.BlockSpec((None, 1, w), lambda bi, ti: (bi, 0, 0)),
        pl.BlockSpec((None, CONV_WIDTH - 1, w), lambda bi, ti: (bi, 0, 0)),
    )
    in_specs = [
        pl.BlockSpec((None, tm, d), lambda bi, ti: (bi, ti, 0)),
        const((1, d)), const(whi.shape), const(wlo.shape), const(cw.shape), const(cb.shape),
        const(wa.shape), const(wx.shape), const(ba.shape), const(bx.shape), const(lam.shape),
    ]
    return pl.pallas_call(
        _prompt_in_kernel,
        grid=(b, s // tm),
        in_specs=in_specs,
        out_specs=out_specs,
        out_shape=out_shape,
        scratch_shapes=[
            pltpu.VMEM((tm + V7X_SUBLANES, w), F32),
            pltpu.VMEM((V7X_SUBLANES, w), F32),
            pltpu.VMEM((tm, w), F32),
            pltpu.VMEM((tm, w), F32),
        ],
        compiler_params=pltpu.CompilerParams(
            dimension_semantics=("arbitrary", "arbitrary"),
            vmem_limit_bytes=VMEM_LIMIT_BYTES),
        name="prompt_in",
    )(x, g_pre, whi, wlo, cw, cb, wa, wx, ba, bx, lam)


def _prompt_attn_kernel(qt_ref, kn_ref, vtb_ref, km_ref, o_ref, w_ref, sel_ref, sa_ref, sb_ref,
                        acc_ref):
    nsb = vtb_ref.shape[0]
    nb = km_ref.shape[0]
    tq = MOBA_BLOCK
    hd = HEAD_DIM
    ones_rows = 2 * V7X_SUBLANES
    first = lax.broadcasted_iota(jnp.int32, (2 * hd, tq), 0) < hd
    kmp = km_ref[...]
    blk = lax.broadcasted_iota(jnp.int32, (nb, tq), 0)
    ones_tile = jnp.ones((ones_rows, tq), BF16)
    causal = (lax.broadcasted_iota(jnp.int32, (tq, tq), 0)
              <= lax.broadcasted_iota(jnp.int32, (tq, tq), 1))

    def stage(sb, c):
        qt = jnp.concatenate([qt_ref[2 * sb], qt_ref[2 * sb + 1]], axis=1)
        for h in range(2):
            qth = jnp.where(first, qt, 0.0) if h == 0 else jnp.where(first, 0.0, qt)
            gate = _dot_exact(kmp, qth)
            gate = jnp.where(blk < sb, gate, NEG_INF)
            sel_ref[sb, h] = _top3_rows(gate, blk, sb)
            w_ref[sb, :, h * tq:(h + 1) * tq] = (qth * (ATT_SCALE * LOG2_E)).astype(BF16)
        return c

    lax.fori_loop(0, nsb, stage, 0, unroll=2)

    def v_lhs(j, h):
        return jnp.concatenate([vtb_ref[j, h * hd:(h + 1) * hd, :], ones_tile], axis=0)

    def key_block(j):
        return kn_ref[pl.ds(pl.multiple_of(j * MOBA_BLOCK, MOBA_BLOCK), MOBA_BLOCK), :]

    heads = (0, 1)
    bufs = (sa_ref, sb_ref)
    hcols = tuple(slice(h * tq, (h + 1) * tq) for h in heads)

    def scores_into(buf, j, wq):
        sc = _dot(key_block(j), wq)
        buf[...] = sc
        return tuple(jnp.max(sc[:, hcols[h]], axis=0, keepdims=True) for h in heads)

    def super_block(sb, c):
        last = sb - 1
        wq = w_ref[sb]
        bufs[0][...] = _dot(key_block(sb), wq)
        cm_first = scores_into(bufs[1], 0, wq)
        m_init = []
        for h in heads:
            s = jnp.where(causal, bufs[0][:, hcols[h]], NEG_INF)
            m = jnp.max(s, axis=0, keepdims=True)
            acc_ref[h] = _dot(v_lhs(sb, h), jnp.exp2(s - m).astype(BF16))
            m_init.append(m)

        def past_block(j, live, cur, nxt, carry):
            m_old, cm = carry
            cm_next = scores_into(bufs[nxt], jnp.minimum(j + 1, last), wq)
            m_out = []
            for h in heads:
                on = sel_ref[sb, h, pl.ds(j, 1), :] * live > 0.0
                m_new = jnp.maximum(m_old[h], jnp.where(on, cm[h], NEG_INF))
                p = jnp.exp2(bufs[cur][:, hcols[h]] - jnp.where(on, m_new, POS_INF)).astype(BF16)
                acc_ref[h] = acc_ref[h] * jnp.exp2(m_old[h] - m_new) + _dot(v_lhs(j, h), p)
                m_out.append(m_new)
            return tuple(m_out), cm_next

        def past_pair(j0, carry):
            j1 = jnp.minimum(j0 + 1, last)
            mid = past_block(j0, 1.0, 1, 0, carry)
            return past_block(j1, jnp.where(j0 + 1 < sb, 1.0, 0.0), 0, 1, mid)

        n_quads = sb // 4
        carry = lax.fori_loop(
            0, n_quads, lambda i, cr: past_pair(4 * i + 2, past_pair(4 * i, cr)),
            (tuple(m_init), cm_first))
        lax.fori_loop(0, (sb - 4 * n_quads + 1) // 2,
                      lambda i, cr: past_pair(4 * n_quads + 2 * i, cr), carry)
        outs = []
        for h in heads:
            a = acc_ref[h]
            outs.append(a[:hd, :] / a[hd:hd + 1, :])
        o_ref[pl.ds(pl.multiple_of(sb * tq, tq), tq), :] = jnp.concatenate(outs, axis=0).T
        return c

    lax.fori_loop(0, nsb, super_block, 0)


def _prompt_attn(qt, kn, vtb, km):
    b, nq, _, qb = qt.shape
    s = kn.shape[1]
    nb = km.shape[1]
    pw = 2 * HEAD_DIM
    tq = MOBA_BLOCK
    return pl.pallas_call(
        _prompt_attn_kernel,
        grid=(b, HEAD_PAIRS),
        in_specs=[
            pl.BlockSpec((None, nq, pw, qb), lambda bi, pi: (bi, 0, pi, 0)),
            pl.BlockSpec((None, s, pw), lambda bi, pi: (bi, 0, pi)),
            pl.BlockSpec((None, nb, pw, tq), lambda bi, pi: (bi, 0, pi, 0)),
            pl.BlockSpec((None, nb, pw), lambda bi, pi: (bi, 0, pi)),
        ],
        out_specs=pl.BlockSpec((None, s, pw), lambda bi, pi: (bi, 0, pi)),
        out_shape=jax.ShapeDtypeStruct((b, s, ATT_WIDTH), F32),
        scratch_shapes=[
            pltpu.VMEM((nb, pw, 2 * tq), BF16),
            pltpu.VMEM((nb, 2, nb, tq), F32),
            pltpu.VMEM((tq, 2 * tq), F32), pltpu.VMEM((tq, 2 * tq), F32),
            pltpu.VMEM((2, HEAD_DIM + 2 * V7X_SUBLANES, tq), F32),
        ],
        compiler_params=pltpu.CompilerParams(
            dimension_semantics=("arbitrary", "arbitrary"),
            vmem_limit_bytes=VMEM_LIMIT_BYTES),
        name="prompt_attn",
    )(qt, kn, vtb, km)


def _mix_out_ab(att, sg, ml, wo_ref):
    ma = (sg.astype(F32) * att).astype(BF16)
    return _dot(ma, wo_ref[:ATT_WIDTH, :]) + _dot(ml.astype(BF16), wo_ref[ATT_WIDTH:, :])


def _gmlp_in(y0, npre_ref, wi_ref, lg_ref, lb_ref):
    xn = _rms_norm(y0, npre_ref[...])
    pr = _dot(xn.astype(BF16), wi_ref[...])
    u = _gelu_tanh(pr[:, :GMLP_WIDTH])
    v = _layer_norm(_gelu_tanh(pr[:, GMLP_WIDTH:2 * GMLP_WIDTH]), lg_ref[...], lb_ref[...])
    g = pr[:, 2 * GMLP_WIDTH:]
    return u, v, g


def _prompt_out_kernel(x_ref, att_ref, sg_ref, ml_ref, np0_ref, wo_ref, npre_ref, wi_ref,
                       lg_ref, lb_ref, ws_ref, bst_ref, wc_ref, np1_ref, y_ref, mix_ref):
    tm = x_ref.shape[0]
    op = _mix_out_ab(att_ref[...], sg_ref[...], ml_ref[...], wo_ref)
    y0 = x_ref[...] + _rms_norm(op, np0_ref[...])
    u, v, g = _gmlp_in(y0, npre_ref, wi_ref, lg_ref, lb_ref)
    vb = v.astype(BF16)
    t_out = lax.broadcasted_iota(jnp.int32, (CHUNK, CHUNK), 0)
    t_in = lax.broadcasted_iota(jnp.int32, (CHUNK, CHUNK), 1)
    for gi in range(GMLP_GROUPS):
        wm = jnp.where(t_in <= t_out, ws_ref[gi], 0.0).astype(BF16)
        cols = slice(gi * GMLP_GROUP_DIM, (gi + 1) * GMLP_GROUP_DIM)
        for c in range(tm // CHUNK):
            rows = slice(c * CHUNK, (c + 1) * CHUNK)
            mix_ref[rows, cols] = _dot(wm, vb[rows, cols]) + bst_ref[:, cols]
    z = _silu(g) * (u * mix_ref[...])
    op1 = _dot(z.astype(BF16), wc_ref[...])
    y_ref[...] = y0 + _rms_norm(op1, np1_ref[...])


def _prompt_out(x2, att2, sg2, ml2, np0, wo, npre1, wi, lg, lb, ws, bst, wc, np1, tm):
    n, d = x2.shape
    const = lambda shape: pl.BlockSpec(shape, lambda i: (0,) * len(shape))
    rows = lambda width: pl.BlockSpec((tm, width), lambda i: (i, 0))
    return pl.pallas_call(
        _prompt_out_kernel,
        grid=(n // tm,),
        in_specs=[rows(d), rows(ATT_WIDTH), rows(ATT_WIDTH), rows(LRU_WIDTH),
                  const(np0.shape), const(wo.shape), const(npre1.shape), const(wi.shape),
                  const(lg.shape), const(lb.shape), const(ws.shape), const(bst.shape),
                  const(wc.shape), const(np1.shape)],
        out_specs=rows(d),
        out_shape=jax.ShapeDtypeStruct((n, d), F32),
        scratch_shapes=[pltpu.VMEM((tm, GMLP_WIDTH), F32)],
        compiler_params=pltpu.CompilerParams(
            dimension_semantics=("arbitrary",),
            vmem_limit_bytes=VMEM_LIMIT_BYTES),
        name="prompt_out",
    )(x2, att2, sg2, ml2, np0, wo, npre1, wi, lg, lb, ws, bst, wc, np1)


def _sample_in_kernel(x_ref, g_ref, whi_ref, wlo_ref, cw_ref, cb_ref, wa_ref, wx_ref,
                      ba_ref, bx_ref, lam_ref, h0_ref, st_ref,
                      q_ref, k_ref, v_ref, sg_ref, ml_ref, hs_ref, xl_ref):
    n = x_ref.shape[0]
    t_len = V7X_SUBLANES
    w = LRU_WIDTH
    xn = _rms_norm(x_ref[...], g_ref[...])
    qk, rest = _in_proj_ab(xn, whi_ref, wlo_ref)
    q_ref[...] = qk[:, :ATT_WIDTH]
    k_ref[...] = qk[:, ATT_WIDTH:]
    v_ref[...] = rest[:, :ATT_WIDTH]
    g_att = rest[:, ATT_WIDTH:2 * ATT_WIDTH]
    x_lru = rest[:, 2 * ATT_WIDTH:2 * ATT_WIDTH + w]
    g_lru = rest[:, 2 * ATT_WIDTH + w:]
    sg_ref[...] = _silu(g_att)
    xl_ref[...] = x_lru

    tok = lax.broadcasted_iota(jnp.int32, (n, w), 0) % t_len
    st = st_ref[...]
    xc = cw_ref[CONV_WIDTH - 1:CONV_WIDTH, :] * x_lru + cb_ref[...]
    for back in range(1, CONV_WIDTH):
        prev = jnp.where(tok >= back, pltpu.roll(x_lru, back, 0),
                         pltpu.roll(st, n - t_len + back, 0))
        xc = xc + cw_ref[CONV_WIDTH - 1 - back:CONV_WIDTH - back, :] * prev

    a, u = _lru_coeffs(xc, wa_ref, wx_ref, ba_ref, bx_ref, lam_ref)
    for d in (1, 2, 4):
        keep = tok >= d
        u = jnp.where(keep, a * pltpu.roll(u, d, 0) + u, u)
        a = jnp.where(keep, a * pltpu.roll(a, d, 0), a)
    hs = a * h0_ref[...] + u
    hs_ref[...] = hs
    ml_ref[...] = _silu(g_lru) * hs


def _sample_in(xs, g_pre, whi, wlo, cw, cb, wa, wx, ba, bx, lam, h0rep, stpad):
    n, d = xs.shape
    w = LRU_WIDTH
    args = (xs, g_pre, whi, wlo, cw, cb, wa, wx, ba, bx, lam, h0rep, stpad)
    full = lambda a: pl.BlockSpec(a.shape, lambda i: (0,) * a.ndim)
    outs = [jax.ShapeDtypeStruct((n, ATT_WIDTH), F32)] * 4 + [jax.ShapeDtypeStruct((n, w), F32)] * 3
    return pl.pallas_call(
        _sample_in_kernel,
        grid=(1,),
        in_specs=[full(a) for a in args],
        out_specs=tuple(pl.BlockSpec(o.shape, lambda i: (0, 0)) for o in outs),
        out_shape=tuple(outs),
        compiler_params=pltpu.CompilerParams(
            dimension_semantics=("arbitrary",), vmem_limit_bytes=VMEM_LIMIT_BYTES),
        name="sample_in",
    )(*args)


def _sample_select_kernel(pt_ref, q_ref, ptv_ref, *refs, pages_per_step, n_blocks):
    k_refs = refs[:pages_per_step]
    idx_ref = refs[pages_per_step]
    kmt = refs[pages_per_step + 1]
    c = pl.program_id(1)
    bp = MOBA_BLOCK // PAGE_SIZE
    lane3 = lax.broadcasted_iota(jnp.int32, kmt.shape, 2)

    @pl.when(c == 0)
    def _():
        kmt[...] = jnp.zeros(kmt.shape, F32)

    for i in range(pages_per_step // bp):
        tot = k_refs[bp * i][...]
        for pg in range(1, bp):
            tot = tot + k_refs[bp * i + pg][...]
        col = jnp.sum(tot, axis=-1, keepdims=True) * (1.0 / MOBA_BLOCK)
        blk = c * (pages_per_step // bp) + i
        kmt[...] = jnp.where(lane3 == blk, col, kmt[...])

    @pl.when(c == pl.num_programs(1) - 1)
    def _():
        qv = q_ref[...]
        t_len = qv.shape[0]
        lane = lax.broadcasted_iota(jnp.int32, (t_len, V7X_LANES), 1)
        lane_pair = lax.broadcasted_iota(jnp.int32, (t_len, 2 * HEAD_DIM), 1)
        pages = jnp.broadcast_to(ptv_ref[...].astype(F32), (t_len, V7X_LANES))
        for p in range(HEAD_PAIRS):
            qp = qv[:, p * 2 * HEAD_DIM:(p + 1) * 2 * HEAD_DIM]
            kmp = kmt[2 * p:2 * p + 2].reshape(2 * HEAD_DIM, V7X_LANES)
            for hh in range(2):
                qm = jnp.where((lane_pair < HEAD_DIM) == (hh == 0), qp, 0.0)
                gate = _dot_exact(qm, kmp)
                gate = jnp.where(lane < n_blocks, gate, NEG_INF)
                out = jnp.zeros((t_len, V7X_LANES), jnp.int32)
                for r in range(MOBA_TOPK):
                    mx = jnp.max(gate, axis=-1, keepdims=True)
                    ix = jnp.min(jnp.where(gate == mx, lane, V7X_LANES), axis=-1, keepdims=True)
                    gate = jnp.where(lane == ix, NEG_INF, gate)
                    for pg in range(bp):
                        phys = jnp.sum(jnp.where(lane == ix * bp + pg, pages, 0.0),
                                       axis=-1, keepdims=True)
                        out = jnp.where(lane == r * bp + pg, phys.astype(jnp.int32), out)
                idx_ref[2 * p + hh] = out


def _sample_select(page_table, q_s, cache_t, layer, pages_per_step):
    db, n_pages = page_table.shape
    t_len = q_s.shape[0] // db
    n_blocks = n_pages * PAGE_SIZE // MOBA_BLOCK
    assert n_pages == V7X_LANES and n_pages % pages_per_step == 0

    def page_spec(i):
        return pl.BlockSpec(
            (None, None, ATT_HEADS, HEAD_DIM, PAGE_SIZE),
            lambda b, c, pt: (layer, pt[b * n_pages + c * pages_per_step + i], 0, 0, 0))

    grid_spec = pltpu.PrefetchScalarGridSpec(
        num_scalar_prefetch=1,
        grid=(db, n_pages // pages_per_step),
        in_specs=[pl.BlockSpec((t_len, ATT_WIDTH), lambda b, c, pt: (b, 0)),
                  pl.BlockSpec((None, 1, n_pages), lambda b, c, pt: (b, 0, 0))]
                 + [page_spec(i) for i in range(pages_per_step)],
        out_specs=pl.BlockSpec((None, ATT_HEADS, t_len, V7X_LANES), lambda b, c, pt: (b, 0, 0, 0)),
        scratch_shapes=[pltpu.VMEM((ATT_HEADS, HEAD_DIM, V7X_LANES), F32)],
    )
    return pl.pallas_call(
        functools.partial(_sample_select_kernel, pages_per_step=pages_per_step, n_blocks=n_blocks),
        grid_spec=grid_spec,
        out_shape=jax.ShapeDtypeStruct((db, ATT_HEADS, t_len, V7X_LANES), jnp.int32),
        compiler_params=pltpu.CompilerParams(
            dimension_semantics=("arbitrary", "arbitrary"), vmem_limit_bytes=VMEM_LIMIT_BYTES),
        name="sample_select",
    )(page_table.reshape(-1), q_s, page_table.reshape(db, 1, n_pages),
      *([cache_t] * pages_per_step))


def _sample_attn_kernel(ph_ref, qt_ref, kt_ref, vt_ref, kc_ref, vc_ref, o_ref, kbuf, vbuf, sem,
                        *, t_len, layer):
    n_sel = MOBA_TOPK * (MOBA_BLOCK // PAGE_SIZE)
    n_tiles = t_len * n_sel
    heads = pl.num_programs(1)
    step = pl.program_id(0) * heads + pl.program_id(1)
    n_steps = pl.num_programs(0) * heads
    slot = step % 2

    def tile_copies(step_i, slot_i, i):
        page = ph_ref[step_i * n_tiles + i]
        head = step_i % heads
        return (pltpu.make_async_copy(kc_ref.at[layer, page, head], kbuf.at[slot_i, i], sem.at[0, slot_i]),
                pltpu.make_async_copy(vc_ref.at[layer, page, head], vbuf.at[slot_i, i], sem.at[1, slot_i]))

    def start_all(step_i, slot_i):
        def body(i, c):
            for cp in tile_copies(step_i, slot_i, i):
                cp.start()
            return c
        lax.fori_loop(0, n_tiles, body, 0)

    @pl.when(step == 0)
    def _():
        start_all(0, 0)

    @pl.when(step + 1 < n_steps)
    def _():
        start_all(step + 1, 1 - slot)

    def wait_tile(i, c):
        for cp in tile_copies(step, slot, i):
            cp.wait()
        return c

    lax.fori_loop(0, n_tiles, wait_tile, 0)

    ktn = kt_ref[...]
    vtn = vt_ref[...]
    tok = lax.broadcasted_iota(jnp.int32, (1, t_len), 1)
    for t in range(t_len):
        qc = qt_ref[:, t:t + 1] * ATT_SCALE
        s_own = jnp.sum(ktn * qc, axis=0, keepdims=True)
        s_own = jnp.where(tok <= t, s_own, NEG_INF)
        s_sel = [jnp.sum(kbuf[slot, t * n_sel + i] * qc, axis=0, keepdims=True)
                 for i in range(n_sel)]
        m = jnp.max(s_own, axis=-1, keepdims=True)
        for s in s_sel:
            m = jnp.maximum(m, jnp.max(s, axis=-1, keepdims=True))
        p_own = jnp.exp(s_own - m)
        den = jnp.sum(p_own, axis=-1, keepdims=True)
        acc = None
        for i, s in enumerate(s_sel):
            p = jnp.exp(s - m)
            den = den + jnp.sum(p, axis=-1, keepdims=True)
            pv = vbuf[slot, t * n_sel + i] * p
            acc = pv if acc is None else acc + pv
        o = jnp.sum(acc, axis=-1, keepdims=True) + jnp.sum(vtn * p_own, axis=-1, keepdims=True)
        o_ref[:, t:t + 1] = o / den


def _sample_attn(phys, qt_s, kt_s, vt_s, cache_kt, cache_vt, layer):
    db, _, t_len = qt_s.shape
    n_tiles = t_len * MOBA_TOPK * (MOBA_BLOCK // PAGE_SIZE)
    new_spec = pl.BlockSpec((None, HEAD_DIM, t_len), lambda b, h, ph: (b, h, 0))
    cache_spec = pl.BlockSpec(memory_space=pl.ANY)
    grid_spec = pltpu.PrefetchScalarGridSpec(
        num_scalar_prefetch=1,
        grid=(db, ATT_HEADS),
        in_specs=[new_spec, new_spec, new_spec, cache_spec, cache_spec],
        out_specs=new_spec,
        scratch_shapes=[
            pltpu.VMEM((2, n_tiles, HEAD_DIM, PAGE_SIZE), F32),
            pltpu.VMEM((2, n_tiles, HEAD_DIM, PAGE_SIZE), F32),
            pltpu.SemaphoreType.DMA((2, 2)),
        ],
    )
    return pl.pallas_call(
        functools.partial(_sample_attn_kernel, t_len=t_len, layer=layer),
        grid_spec=grid_spec,
        out_shape=jax.ShapeDtypeStruct((db, ATT_WIDTH, t_len), F32),
        compiler_params=pltpu.CompilerParams(
            dimension_semantics=("arbitrary", "arbitrary"), vmem_limit_bytes=VMEM_LIMIT_BYTES),
        name="sample_attn",
    )(phys.reshape(-1), qt_s, kt_s, vt_s, cache_kt, cache_vt)


def _sample_out_kernel(x_ref, att_ref, sg_ref, ml_ref, np0_ref, wo_ref, npre_ref, wi_ref,
                       lg_ref, lb_ref, cd_ref, bst_ref, wc_ref, np1_ref, y_ref, gv_ref):
    n = x_ref.shape[0]
    t_len = cd_ref.shape[1]
    reps = n // t_len
    op = _mix_out_ab(att_ref[...], sg_ref[...], ml_ref[...], wo_ref)
    y0 = x_ref[...] + _rms_norm(op, np0_ref[...])
    u, v, g = _gmlp_in(y0, npre_ref, wi_ref, lg_ref, lb_ref)
    gv_ref[...] = v
    tile = lambda tab: jnp.concatenate([tab] * reps, axis=0)
    mix = tile(bst_ref[...]) + tile(cd_ref[0]) * v
    for d in range(1, t_len):
        mix = mix + tile(cd_ref[d]) * pltpu.roll(v, d, 0)
    z = _silu(g) * (u * mix)
    op1 = _dot(z.astype(BF16), wc_ref[...])
    y_ref[...] = y0 + _rms_norm(op1, np1_ref[...])


def _sample_out(xs, att, sg, ml, np0, wo, npre1, wi, lg, lb, cd, bst8, wc, np1):
    n, d = xs.shape
    args = (xs, att, sg, ml, np0, wo, npre1, wi, lg, lb, cd, bst8, wc, np1)
    full = lambda a: pl.BlockSpec(a.shape, lambda i: (0,) * a.ndim)
    outs = (jax.ShapeDtypeStruct((n, d), F32), jax.ShapeDtypeStruct((n, GMLP_WIDTH), F32))
    return pl.pallas_call(
        _sample_out_kernel,
        grid=(1,),
        in_specs=[full(a) for a in args],
        out_specs=tuple(pl.BlockSpec(o.shape, lambda i: (0, 0)) for o in outs),
        out_shape=outs,
        compiler_params=pltpu.CompilerParams(
            dimension_semantics=("arbitrary",), vmem_limit_bytes=VMEM_LIMIT_BYTES),
        name="sample_out",
    )(*args)


def _block_diag(wh):
    h, n, _ = wh.shape
    eye = jnp.eye(h, dtype=wh.dtype)
    return jnp.einsum("hij,hg->higj", wh, eye).reshape(h * n, h * n)


def kernel(x_prompt, x_sample, cache_k, cache_v, page_table, state_lru_h, state_conv, norm_pre, norm_post, w_in_ab, conv_w, conv_b, lru_wa, lru_ba, lru_wx, lru_bx, lru_lambda, w_out_ab, w_in_c, c_ln_g, c_ln_b, c_ws, c_bs, w_out_c):
    b, s, d = x_prompt.shape
    db, t_len, _ = x_sample.shape
    n_pages = page_table.shape[1]
    assert norm_pre.shape[0] == 2 and w_in_ab.shape[0] == 1 and w_in_c.shape[0] == 1
    assert s % (2 * MOBA_BLOCK) == 0 and t_len == V7X_SUBLANES
    assert (n_pages * PAGE_SIZE) % MOBA_BLOCK == 0 and t_len <= CHUNK
    assert cache_k.shape[2:] == (PAGE_SIZE, ATT_HEADS, HEAD_DIM)
    w = LRU_WIDTH
    row = lambda vec: vec.reshape(1, -1)

    whi = w_in_ab[0].astype(BF16)
    wq = w_in_ab[0][:, :2 * ATT_WIDTH]
    wlo = (wq - wq.astype(BF16).astype(F32)).astype(BF16)
    wa = _block_diag(lru_wa[0]).astype(BF16)
    wx = _block_diag(lru_wx[0]).astype(BF16)
    lru_args = (conv_w[0], row(conv_b[0]), wa, wx, row(lru_ba[0]), row(lru_bx[0]), row(lru_lambda[0]))
    wo = w_out_ab[0].astype(BF16)
    wi = w_in_c[0].astype(BF16)
    wc = w_out_c[0].astype(BF16)
    np0, np1 = row(norm_post[0]), row(norm_post[1])
    npre0, npre1 = row(norm_pre[0]), row(norm_pre[1])
    lg, lb = row(c_ln_g[0]), row(c_ln_b[0])
    bst = jnp.repeat(c_bs[0].T, GMLP_GROUP_DIM, axis=1)

    (qt, kt, vt, vtb, kn, km, sg, ml, h_last, conv_last) = _prompt_in(
        x_prompt, npre0, whi, wlo, *lru_args, tm=2 * MOBA_BLOCK)
    att = _prompt_attn(qt, kn, vtb, km)
    y_prompt = _prompt_out(
        x_prompt.reshape(b * s, d), att.reshape(b * s, ATT_WIDTH), sg.reshape(b * s, ATT_WIDTH),
        ml.reshape(b * s, w), np0, wo, npre1, wi, lg, lb, c_ws[0], bst, wc, np1,
        tm=MOBA_BLOCK).reshape(b, s, d)
    heads_last = lambda xt: xt.reshape(b, ATT_HEADS, HEAD_DIM, s).transpose(0, 3, 1, 2)[None]
    k_prompt, v_prompt = heads_last(kt), heads_last(vt)

    n = db * t_len
    xs = x_sample.reshape(n, d)
    h0rep = jnp.repeat(state_lru_h[0], t_len, axis=0)
    stpad = jnp.pad(state_conv[0], ((0, 0), (t_len - (CONV_WIDTH - 1), 0), (0, 0))).reshape(n, w)
    q_s, k_s, v_s, sg_s, ml_s, hs_s, xl_s = _sample_in(xs, npre0, whi, wlo, *lru_args, h0rep, stpad)

    cache_kt = cache_k.transpose(0, 1, 3, 4, 2)
    cache_vt = cache_v.transpose(0, 1, 3, 4, 2)
    phys = _sample_select(page_table, q_s, cache_kt, 0, pages_per_step=16)
    tok_last = lambda a: a.reshape(db, t_len, ATT_WIDTH).transpose(0, 2, 1)
    att_t = _sample_attn(phys[..., :MOBA_TOPK * (MOBA_BLOCK // PAGE_SIZE)], tok_last(q_s),
                         tok_last(k_s), tok_last(v_s), cache_kt, cache_vt, 0)
    att_s = att_t.transpose(0, 2, 1).reshape(n, ATT_WIDTH)

    ws8 = c_ws[0][:, :t_len, :t_len]
    tt = jnp.arange(t_len)
    diag = lambda dd: jnp.where(tt >= dd, ws8[:, tt, jnp.maximum(tt - dd, 0)], 0.0)
    cd = jnp.stack([jnp.repeat(diag(dd).T, GMLP_GROUP_DIM, axis=1) for dd in range(t_len)])
    y_s, gv_s = _sample_out(xs, att_s, sg_s, ml_s, np0, wo, npre1, wi, lg, lb, cd, bst[:t_len],
                            wc, np1)

    per_req = lambda a, width: a.reshape(db, t_len, width)
    return (y_prompt, y_s.reshape(db, t_len, d), k_prompt, v_prompt,
            h_last.reshape(1, b, w), conv_last[None],
            k_s.reshape(1, db, t_len, ATT_HEADS, HEAD_DIM), v_s.reshape(1, db, t_len, ATT_HEADS, HEAD_DIM),
            per_req(hs_s, w)[:, t_len - 1][None], per_req(xl_s, w)[:, t_len - (CONV_WIDTH - 1):][None],
            per_req(gv_s, GMLP_WIDTH)[None])
```

```python
import functools

import jax
import jax.numpy as jnp
from jax import lax
from jax.experimental import pallas as pl
from jax.experimental.pallas import tpu as pltpu

ATT_HEADS = 8
HEAD_DIM = 64
ATT_WIDTH = ATT_HEADS * HEAD_DIM
HEAD_PAIRS = ATT_HEADS // 2
MOBA_BLOCK = 256
MOBA_TOPK = 3
QUERY_BLOCK = 128
LRU_WIDTH = 512
CONV_WIDTH = 4
LRU_C = 8.0
GMLP_WIDTH = 1024
GMLP_GROUPS = 8
GMLP_GROUP_DIM = GMLP_WIDTH // GMLP_GROUPS
CHUNK = 128
PAGE_SIZE = 128
NORM_EPS = 1e-6
ATT_SCALE = HEAD_DIM ** -0.5
LOG2_E = 1.4426950408889634

V7X_LANES = 128
V7X_SUBLANES = 8
VMEM_LIMIT_BYTES = 56 * 1024 * 1024

F32 = jnp.float32
BF16 = jnp.bfloat16
NEG_INF = float("-inf")
POS_INF = float("inf")


def _rms_norm(x, g):
    return x * lax.rsqrt(jnp.mean(x * x, axis=-1, keepdims=True) + NORM_EPS) * g


def _layer_norm(x, g, b):
    mu = jnp.mean(x, axis=-1, keepdims=True)
    xc = x - mu
    var = jnp.mean(xc * xc, axis=-1, keepdims=True)
    return xc * lax.rsqrt(var + NORM_EPS) * g + b


def _sigmoid(x):
    return 1.0 / (1.0 + jnp.exp(-x))


def _silu(x):
    return x * _sigmoid(x)


def _gelu_tanh(x):
    c = 0.7978845608028654
    return 0.5 * x * (1.0 + jnp.tanh(c * (x + 0.044715 * (x * x * x))))


def _softplus(x):
    return jnp.maximum(x, 0.0) + jnp.log1p(jnp.exp(-jnp.abs(x)))


def _dot(a, b):
    return jnp.dot(a, b, preferred_element_type=F32)


def _dot_exact(a, b):
    return jnp.dot(a, b, preferred_element_type=F32, precision=lax.Precision.HIGHEST)


def _split_bf16(x):
    hi = x.astype(BF16)
    lo = (x - hi.astype(F32)).astype(BF16)
    return hi, lo


def _in_proj_ab(xn, whi_ref, wlo_ref):
    qk_w = 2 * ATT_WIDTH
    xh, xl = _split_bf16(xn)
    qk = (_dot(xh, whi_ref[:, :qk_w]) + _dot(xh, wlo_ref[...])
          + _dot(xl, whi_ref[:, :qk_w]))
    rest = _dot(xh, whi_ref[:, qk_w:])
    return qk, rest


def _lru_coeffs(xc, wa_ref, wx_ref, ba_ref, bx_ref, lam_ref):
    xcb = xc.astype(BF16)
    r = _sigmoid(_dot(xcb, wa_ref[...]) + ba_ref[...])
    i = _sigmoid(_dot(xcb, wx_ref[...]) + bx_ref[...])
    log_a = (-LRU_C) * r * _softplus(-lam_ref[...])
    a = jnp.exp(log_a)
    u = jnp.sqrt(-jnp.tanh(log_a) * (a * a + 1.0)) * (i * xc)
    return a, u


def _top3_rows(gate, blk, n_valid):
    nb = gate.shape[0]
    sel = jnp.zeros(gate.shape, F32)
    for r in range(MOBA_TOPK):
        mx = jnp.max(gate, axis=0, keepdims=True)
        ix = jnp.min(jnp.where(gate == mx, blk, nb), axis=0, keepdims=True)
        hit = blk == ix
        sel = jnp.maximum(sel, jnp.where(hit, jnp.where(r < n_valid, 1.0, 0.0), 0.0))
        gate = jnp.where(hit, NEG_INF, gate)
    return sel


def _prompt_in_kernel(x_ref, g_ref, whi_ref, wlo_ref, cw_ref, cb_ref, wa_ref, wx_ref,
                      ba_ref, bx_ref, lam_ref,
                      qt_ref, kt_ref, vt_ref, vtb_ref, kn_ref, km_ref, sg_ref, ml_ref,
                      hl_ref, cl_ref,
                      xbuf, hcar, abuf, ubuf):
    t = pl.program_id(1)
    tm = x_ref.shape[0]
    w = LRU_WIDTH

    @pl.when(t == 0)
    def _():
        xbuf[0:V7X_SUBLANES, :] = jnp.zeros((V7X_SUBLANES, w), F32)
        hcar[...] = jnp.zeros(hcar.shape, F32)

    xn = _rms_norm(x_ref[...], g_ref[...])
    qk, rest = _in_proj_ab(xn, whi_ref, wlo_ref)
    q = qk[:, :ATT_WIDTH]
    k = qk[:, ATT_WIDTH:]
    v = rest[:, :ATT_WIDTH]
    g_att = rest[:, ATT_WIDTH:2 * ATT_WIDTH]
    x_lru = rest[:, 2 * ATT_WIDTH:2 * ATT_WIDTH + w]
    g_lru = rest[:, 2 * ATT_WIDTH + w:]

    for s in range(tm // QUERY_BLOCK):
        rows = slice(s * QUERY_BLOCK, (s + 1) * QUERY_BLOCK)
        qt_ref[s] = q[rows, :].T
    kt_ref[...] = k.T
    vt = v.T
    vt_ref[...] = vt
    kn_ref[...] = k.astype(BF16)
    for s in range(tm // MOBA_BLOCK):
        vtb_ref[s] = vt[:, s * MOBA_BLOCK:(s + 1) * MOBA_BLOCK].astype(BF16)
        km_ref[pl.ds(t * (tm // MOBA_BLOCK) + s, 1), :] = jnp.mean(
            k[s * MOBA_BLOCK:(s + 1) * MOBA_BLOCK, :], axis=0, keepdims=True)
    sg_ref[...] = _silu(g_att).astype(BF16)

    xbuf[V7X_SUBLANES:, :] = x_lru
    xc = cw_ref[CONV_WIDTH - 1:CONV_WIDTH, :] * x_lru + cb_ref[...]
    for back in range(1, CONV_WIDTH):
        xc = xc + (cw_ref[CONV_WIDTH - 1 - back:CONV_WIDTH - back, :]
                   * xbuf[pl.ds(V7X_SUBLANES - back, tm), :])
    xbuf[0:V7X_SUBLANES, :] = x_lru[tm - V7X_SUBLANES:, :]

    a, u = _lru_coeffs(xc, wa_ref, wx_ref, ba_ref, bx_ref, lam_ref)
    abuf[...] = a
    ubuf[...] = u

    row = lax.broadcasted_iota(jnp.int32, (V7X_SUBLANES, w), 0)

    def group(gi, h):
        r0 = pl.multiple_of(gi * V7X_SUBLANES, V7X_SUBLANES)
        ag = abuf[pl.ds(r0, V7X_SUBLANES), :]
        ug = ubuf[pl.ds(r0, V7X_SUBLANES), :]
        for d in (1, 2, 4):
            keep = row >= d
            ug = jnp.where(keep, ag * pltpu.roll(ug, d, 0) + ug, ug)
            ag = jnp.where(keep, ag * pltpu.roll(ag, d, 0), ag)
        hg = ag * h + ug
        ubuf[pl.ds(r0, V7X_SUBLANES), :] = hg
        return jnp.broadcast_to(hg[V7X_SUBLANES - 1:, :], (V7X_SUBLANES, w))

    hcar[...] = lax.fori_loop(0, tm // V7X_SUBLANES, group, hcar[...])
    hs = ubuf[...]
    ml_ref[...] = (_silu(g_lru) * hs).astype(BF16)

    @pl.when(t == pl.num_programs(1) - 1)
    def _():
        hl_ref[...] = hs[tm - 1:, :]
        cl_ref[...] = x_lru[tm - (CONV_WIDTH - 1):, :]


def _prompt_in(x, g_pre, whi, wlo, cw, cb, wa, wx, ba, bx, lam, tm):
    b, s, d = x.shape
    w = LRU_WIDTH
    nb = s // MOBA_BLOCK
    nq = s // QUERY_BLOCK
    const = lambda shape: pl.BlockSpec(shape, lambda bi, ti: (0,) * len(shape))
    out_shape = (
        jax.ShapeDtypeStruct((b, nq, ATT_WIDTH, QUERY_BLOCK), F32),
        jax.ShapeDtypeStruct((b, ATT_WIDTH, s), F32),
        jax.ShapeDtypeStruct((b, ATT_WIDTH, s), F32),
        jax.ShapeDtypeStruct((b, nb, ATT_WIDTH, MOBA_BLOCK), BF16),
        jax.ShapeDtypeStruct((b, s, ATT_WIDTH), BF16),
        jax.ShapeDtypeStruct((b, nb, ATT_WIDTH), F32),
        jax.ShapeDtypeStruct((b, s, ATT_WIDTH), BF16),
        jax.ShapeDtypeStruct((b, s, w), BF16),
        jax.ShapeDtypeStruct((b, 1, w), F32),
        jax.ShapeDtypeStruct((b, CONV_WIDTH - 1, w), F32),
    )
    out_specs = (
        pl.BlockSpec((None, tm // QUERY_BLOCK, ATT_WIDTH, QUERY_BLOCK), lambda bi, ti: (bi, ti, 0, 0)),
        pl.BlockSpec((None, ATT_WIDTH, tm), lambda bi, ti: (bi, 0, ti)),
        pl.BlockSpec((None, ATT_WIDTH, tm), lambda bi, ti: (bi, 0, ti)),
        pl.BlockSpec((None, tm // MOBA_BLOCK, ATT_WIDTH, MOBA_BLOCK), lambda bi, ti: (bi, ti, 0, 0)),
        pl.BlockSpec((None, tm, ATT_WIDTH), lambda bi, ti: (bi, ti, 0)),
        pl.BlockSpec((None, nb, ATT_WIDTH), lambda bi, ti: (bi, 0, 0)),
        pl.BlockSpec((None, tm, ATT_WIDTH), lambda bi, ti: (bi, ti, 0)),
        pl.BlockSpec((None, tm, w), lambda bi, ti: (bi, ti, 0)),
        pl.BlockSpec((None, 1, w), lambda bi, ti: (bi, 0, 0)),
        pl.BlockSpec((None, CONV_WIDTH - 1, w), lambda bi, ti: (bi, 0, 0)),
    )
    in_specs = [
        pl.BlockSpec((None, tm, d), lambda bi, ti: (bi, ti, 0)),
        const((1, d)), const(whi.shape), const(wlo.shape), const(cw.shape), const(cb.shape),
        const(wa.shape), const(wx.shape), const(ba.shape), const(bx.shape), const(lam.shape),
    ]
    return pl.pallas_call(
        _prompt_in_kernel,
        grid=(b, s // tm),
        in_specs=in_specs,
        out_specs=out_specs,
        out_shape=out_shape,
        scratch_shapes=[
            pltpu.VMEM((tm + V7X_SUBLANES, w), F32),
            pltpu.VMEM((V7X_SUBLANES, w), F32),
            pltpu.VMEM((tm, w), F32),
            pltpu.VMEM((tm, w), F32),
        ],
        compiler_params=pltpu.CompilerParams(
            dimension_semantics=("arbitrary", "arbitrary"),
            vmem_limit_bytes=VMEM_LIMIT_BYTES),
        name="prompt_in",
    )(x, g_pre, whi, wlo, cw, cb, wa, wx, ba, bx, lam)


def _prompt_attn_kernel(qt_ref, kn_ref, vtb_ref, km_ref, o_ref, w_ref, sel_ref, sa_ref, sb_ref,
                        acc_ref):
    nsb = vtb_ref.shape[0]
    nb = km_ref.shape[0]
    tq = MOBA_BLOCK
    hd = HEAD_DIM
    ones_rows = 2 * V7X_SUBLANES
    first = lax.broadcasted_iota(jnp.int32, (2 * hd, tq), 0) < hd
    kmp = km_ref[...]
    blk = lax.broadcasted_iota(jnp.int32, (nb, tq), 0)
    ones_tile = jnp.ones((ones_rows, tq), BF16)
    causal = (lax.broadcasted_iota(jnp.int32, (tq, tq), 0)
              <= lax.broadcasted_iota(jnp.int32, (tq, tq), 1))

    def stage(sb, c):
        qt = jnp.concatenate([qt_ref[2 * sb], qt_ref[2 * sb + 1]], axis=1)
        for h in range(2):
            qth = jnp.where(first, qt, 0.0) if h == 0 else jnp.where(first, 0.0, qt)
            gate = _dot_exact(kmp, qth)
            gate = jnp.where(blk < sb, gate, NEG_INF)
            sel_ref[sb, h] = _top3_rows(gate, blk, sb)
            w_ref[sb, :, h * tq:(h + 1) * tq] = (qth * (ATT_SCALE * LOG2_E)).astype(BF16)
        return c

    lax.fori_loop(0, nsb, stage, 0, unroll=2)

    def v_lhs(j, h):
        return jnp.concatenate([vtb_ref[j, h * hd:(h + 1) * hd, :], ones_tile], axis=0)

    def key_block(j):
        return kn_ref[pl.ds(pl.multiple_of(j * MOBA_BLOCK, MOBA_BLOCK), MOBA_BLOCK), :]

    heads = (0, 1)
    bufs = (sa_ref, sb_ref)
    hcols = tuple(slice(h * tq, (h + 1) * tq) for h in heads)

    def scores_into(buf, j, wq):
        sc = _dot(key_block(j), wq)
        buf[...] = sc
        return tuple(jnp.max(sc[:, hcols[h]], axis=0, keepdims=True) for h in heads)

    def super_block(sb, c):
        last = sb - 1
        wq = w_ref[sb]
        bufs[0][...] = _dot(key_block(sb), wq)
        cm_first = scores_into(bufs[1], 0, wq)
        m_init = []
        for h in heads:
            s = jnp.where(causal, bufs[0][:, hcols[h]], NEG_INF)
            m = jnp.max(s, axis=0, keepdims=True)
            acc_ref[h] = _dot(v_lhs(sb, h), jnp.exp2(s - m).astype(BF16))
            m_init.append(m)

        def past_block(j, live, cur, nxt, carry):
            m_old, cm = carry
            cm_next = scores_into(bufs[nxt], jnp.minimum(j + 1, last), wq)
            m_out = []
            for h in heads:
                on = sel_ref[sb, h, pl.ds(j, 1), :] * live > 0.0
                m_new = jnp.maximum(m_old[h], jnp.where(on, cm[h], NEG_INF))
                p = jnp.exp2(bufs[cur][:, hcols[h]] - jnp.where(on, m_new, POS_INF)).astype(BF16)
                acc_ref[h] = acc_ref[h] * jnp.exp2(m_old[h] - m_new) + _dot(v_lhs(j, h), p)
                m_out.append(m_new)
            return tuple(m_out), cm_next

        def past_pair(j0, carry):
            j1 = jnp.minimum(j0 + 1, last)
            mid = past_block(j0, 1.0, 1, 0, carry)
            return past_block(j1, jnp.where(j0 + 1 < sb, 1.0, 0.0), 0, 1, mid)

        def past_run(j0, n_pairs, cr):
            for k in range(n_pairs):
                cr = past_pair(j0 + 2 * k, cr)
            return cr

        n_long = sb // 8
        carry = lax.fori_loop(0, n_long, lambda i, cr: past_run(8 * i, 4, cr),
                              (tuple(m_init), cm_first))
        done = 8 * n_long
        n_mid = (sb - done) // 4
        carry = lax.fori_loop(0, n_mid, lambda i, cr: past_run(done + 4 * i, 2, cr), carry)
        done = done + 4 * n_mid
        lax.fori_loop(0, (sb - done + 1) // 2, lambda i, cr: past_pair(done + 2 * i, cr), carry)
        outs = []
        for h in heads:
            a = acc_ref[h]
            outs.append(a[:hd, :] / a[hd:hd + 1, :])
        o_ref[pl.ds(pl.multiple_of(sb * tq, tq), tq), :] = jnp.concatenate(outs, axis=0).T
        return c

    lax.fori_loop(0, nsb, super_block, 0)


def _prompt_attn(qt, kn, vtb, km):
    b, nq, _, qb = qt.shape
    s = kn.shape[1]
    nb = km.shape[1]
    pw = 2 * HEAD_DIM
    tq = MOBA_BLOCK
    return pl.pallas_call(
        _prompt_attn_kernel,
        grid=(b, HEAD_PAIRS),
        in_specs=[
            pl.BlockSpec((None, nq, pw, qb), lambda bi, pi: (bi, 0, pi, 0)),
            pl.BlockSpec((None, s, pw), lambda bi, pi: (bi, 0, pi)),
            pl.BlockSpec((None, nb, pw, tq), lambda bi, pi: (bi, 0, pi, 0)),
            pl.BlockSpec((None, nb, pw), lambda bi, pi: (bi, 0, pi)),
        ],
        out_specs=pl.BlockSpec((None, s, pw), lambda bi, pi: (bi, 0, pi)),
        out_shape=jax.ShapeDtypeStruct((b, s, ATT_WIDTH), F32),
        scratch_shapes=[
            pltpu.VMEM((nb, pw, 2 * tq), BF16),
            pltpu.VMEM((nb, 2, nb, tq), F32),
            pltpu.VMEM((tq, 2 * tq), F32), pltpu.VMEM((tq, 2 * tq), F32),
            pltpu.VMEM((2, HEAD_DIM + 2 * V7X_SUBLANES, tq), F32),
        ],
        compiler_params=pltpu.CompilerParams(
            dimension_semantics=("arbitrary", "arbitrary"),
            vmem_limit_bytes=VMEM_LIMIT_BYTES),
        name="prompt_attn",
    )(qt, kn, vtb, km)


def _mix_out_ab(att, sg, ml, wo_ref):
    ma = (sg.astype(F32) * att).astype(BF16)
    return _dot(ma, wo_ref[:ATT_WIDTH, :]) + _dot(ml.astype(BF16), wo_ref[ATT_WIDTH:, :])


def _gmlp_in(y0, npre_ref, wi_ref, lg_ref, lb_ref):
    xn = _rms_norm(y0, npre_ref[...])
    pr = _dot(xn.astype(BF16), wi_ref[...])
    u = _gelu_tanh(pr[:, :GMLP_WIDTH])
    v = _layer_norm(_gelu_tanh(pr[:, GMLP_WIDTH:2 * GMLP_WIDTH]), lg_ref[...], lb_ref[...])
    g = pr[:, 2 * GMLP_WIDTH:]
    return u, v, g


def _prompt_out_kernel(x_ref, att_ref, sg_ref, ml_ref, np0_ref, wo_ref, npre_ref, wi_ref,
                       lg_ref, lb_ref, ws_ref, bst_ref, wc_ref, np1_ref, y_ref, mix_ref):
    tm = x_ref.shape[0]
    op = _mix_out_ab(att_ref[...], sg_ref[...], ml_ref[...], wo_ref)
    y0 = x_ref[...] + _rms_norm(op, np0_ref[...])
    u, v, g = _gmlp_in(y0, npre_ref, wi_ref, lg_ref, lb_ref)
    vb = v.astype(BF16)
    t_out = lax.broadcasted_iota(jnp.int32, (CHUNK, CHUNK), 0)
    t_in = lax.broadcasted_iota(jnp.int32, (CHUNK, CHUNK), 1)
    for gi in range(GMLP_GROUPS):
        wm = jnp.where(t_in <= t_out, ws_ref[gi], 0.0).astype(BF16)
        cols = slice(gi * GMLP_GROUP_DIM, (gi + 1) * GMLP_GROUP_DIM)
        for c in range(tm // CHUNK):
            rows = slice(c * CHUNK, (c + 1) * CHUNK)
            mix_ref[rows, cols] = _dot(wm, vb[rows, cols]) + bst_ref[:, cols]
    z = _silu(g) * (u * mix_ref[...])
    op1 = _dot(z.astype(BF16), wc_ref[...])
    y_ref[...] = y0 + _rms_norm(op1, np1_ref[...])


def _prompt_out(x2, att2, sg2, ml2, np0, wo, npre1, wi, lg, lb, ws, bst, wc, np1, tm):
    n, d = x2.shape
    const = lambda shape: pl.BlockSpec(shape, lambda i: (0,) * len(shape))
    rows = lambda width: pl.BlockSpec((tm, width), lambda i: (i, 0))
    return pl.pallas_call(
        _prompt_out_kernel,
        grid=(n // tm,),
        in_specs=[rows(d), rows(ATT_WIDTH), rows(ATT_WIDTH), rows(LRU_WIDTH),
                  const(np0.shape), const(wo.shape), const(npre1.shape), const(wi.shape),
                  const(lg.shape), const(lb.shape), const(ws.shape), const(bst.shape),
                  const(wc.shape), const(np1.shape)],
        out_specs=rows(d),
        out_shape=jax.ShapeDtypeStruct((n, d), F32),
        scratch_shapes=[pltpu.VMEM((tm, GMLP_WIDTH), F32)],
        compiler_params=pltpu.CompilerParams(
            dimension_semantics=("arbitrary",),
            vmem_limit_bytes=VMEM_LIMIT_BYTES),
        name="prompt_out",
    )(x2, att2, sg2, ml2, np0, wo, npre1, wi, lg, lb, ws, bst, wc, np1)


def _sample_in_kernel(x_ref, g_ref, whi_ref, wlo_ref, cw_ref, cb_ref, wa_ref, wx_ref,
                      ba_ref, bx_ref, lam_ref, h0_ref, st_ref,
                      q_ref, k_ref, v_ref, sg_ref, ml_ref, hs_ref, xl_ref):
    n = x_ref.shape[0]
    t_len = V7X_SUBLANES
    w = LRU_WIDTH
    xn = _rms_norm(x_ref[...], g_ref[...])
    qk, rest = _in_proj_ab(xn, whi_ref, wlo_ref)
    q_ref[...] = qk[:, :ATT_WIDTH]
    k_ref[...] = qk[:, ATT_WIDTH:]
    v_ref[...] = rest[:, :ATT_WIDTH]
    g_att = rest[:, ATT_WIDTH:2 * ATT_WIDTH]
    x_lru = rest[:, 2 * ATT_WIDTH:2 * ATT_WIDTH + w]
    g_lru = rest[:, 2 * ATT_WIDTH + w:]
    sg_ref[...] = _silu(g_att)
    xl_ref[...] = x_lru

    tok = lax.broadcasted_iota(jnp.int32, (n, w), 0) % t_len
    st = st_ref[...]
    xc = cw_ref[CONV_WIDTH - 1:CONV_WIDTH, :] * x_lru + cb_ref[...]
    for back in range(1, CONV_WIDTH):
        prev = jnp.where(tok >= back, pltpu.roll(x_lru, back, 0),
                         pltpu.roll(st, n - t_len + back, 0))
        xc = xc + cw_ref[CONV_WIDTH - 1 - back:CONV_WIDTH - back, :] * prev

    a, u = _lru_coeffs(xc, wa_ref, wx_ref, ba_ref, bx_ref, lam_ref)
    for d in (1, 2, 4):
        keep = tok >= d
        u = jnp.where(keep, a * pltpu.roll(u, d, 0) + u, u)
        a = jnp.where(keep, a * pltpu.roll(a, d, 0), a)
    hs = a * h0_ref[...] + u
    hs_ref[...] = hs
    ml_ref[...] = _silu(g_lru) * hs


def _sample_in(xs, g_pre, whi, wlo, cw, cb, wa, wx, ba, bx, lam, h0rep, stpad):
    n, d = xs.shape
    w = LRU_WIDTH
    args = (xs, g_pre, whi, wlo, cw, cb, wa, wx, ba, bx, lam, h0rep, stpad)
    full = lambda a: pl.BlockSpec(a.shape, lambda i: (0,) * a.ndim)
    outs = [jax.ShapeDtypeStruct((n, ATT_WIDTH), F32)] * 4 + [jax.ShapeDtypeStruct((n, w), F32)] * 3
    return pl.pallas_call(
        _sample_in_kernel,
        grid=(1,),
        in_specs=[full(a) for a in args],
        out_specs=tuple(pl.BlockSpec(o.shape, lambda i: (0, 0)) for o in outs),
        out_shape=tuple(outs),
        compiler_params=pltpu.CompilerParams(
            dimension_semantics=("arbitrary",), vmem_limit_bytes=VMEM_LIMIT_BYTES),
        name="sample_in",
    )(*args)


def _sample_select_kernel(pt_ref, q_ref, ptv_ref, *refs, pages_per_step, n_blocks):
    k_refs = refs[:pages_per_step]
    idx_ref = refs[pages_per_step]
    kmt = refs[pages_per_step + 1]
    c = pl.program_id(1)
    bp = MOBA_BLOCK // PAGE_SIZE
    lane3 = lax.broadcasted_iota(jnp.int32, kmt.shape, 2)

    @pl.when(c == 0)
    def _():
        kmt[...] = jnp.zeros(kmt.shape, F32)

    for i in range(pages_per_step // bp):
        tot = k_refs[bp * i][...]
        for pg in range(1, bp):
            tot = tot + k_refs[bp * i + pg][...]
        col = jnp.sum(tot, axis=-1, keepdims=True) * (1.0 / MOBA_BLOCK)
        blk = c * (pages_per_step // bp) + i
        kmt[...] = jnp.where(lane3 == blk, col, kmt[...])

    @pl.when(c == pl.num_programs(1) - 1)
    def _():
        qv = q_ref[...]
        t_len = qv.shape[0]
        lane = lax.broadcasted_iota(jnp.int32, (t_len, V7X_LANES), 1)
        lane_pair = lax.broadcasted_iota(jnp.int32, (t_len, 2 * HEAD_DIM), 1)
        pages = jnp.broadcast_to(ptv_ref[...].astype(F32), (t_len, V7X_LANES))
        for p in range(HEAD_PAIRS):
            qp = qv[:, p * 2 * HEAD_DIM:(p + 1) * 2 * HEAD_DIM]
            kmp = kmt[2 * p:2 * p + 2].reshape(2 * HEAD_DIM, V7X_LANES)
            for hh in range(2):
                qm = jnp.where((lane_pair < HEAD_DIM) == (hh == 0), qp, 0.0)
                gate = _dot_exact(qm, kmp)
                gate = jnp.where(lane < n_blocks, gate, NEG_INF)
                out = jnp.zeros((t_len, V7X_LANES), jnp.int32)
                for r in range(MOBA_TOPK):
                    mx = jnp.max(gate, axis=-1, keepdims=True)
                    ix = jnp.min(jnp.where(gate == mx, lane, V7X_LANES), axis=-1, keepdims=True)
                    gate = jnp.where(lane == ix, NEG_INF, gate)
                    for pg in range(bp):
                        phys = jnp.sum(jnp.where(lane == ix * bp + pg, pages, 0.0),
                                       axis=-1, keepdims=True)
                        out = jnp.where(lane == r * bp + pg, phys.astype(jnp.int32), out)
                idx_ref[2 * p + hh] = out


def _sample_select(page_table, q_s, cache_t, layer, pages_per_step):
    db, n_pages = page_table.shape
    t_len = q_s.shape[0] // db
    n_blocks = n_pages * PAGE_SIZE // MOBA_BLOCK
    assert n_pages == V7X_LANES and n_pages % pages_per_step == 0

    def page_spec(i):
        return pl.BlockSpec(
            (None, None, ATT_HEADS, HEAD_DIM, PAGE_SIZE),
            lambda b, c, pt: (layer, pt[b * n_pages + c * pages_per_step + i], 0, 0, 0))

    grid_spec = pltpu.PrefetchScalarGridSpec(
        num_scalar_prefetch=1,
        grid=(db, n_pages // pages_per_step),
        in_specs=[pl.BlockSpec((t_len, ATT_WIDTH), lambda b, c, pt: (b, 0)),
                  pl.BlockSpec((None, 1, n_pages), lambda b, c, pt: (b, 0, 0))]
                 + [page_spec(i) for i in range(pages_per_step)],
        out_specs=pl.BlockSpec((None, ATT_HEADS, t_len, V7X_LANES), lambda b, c, pt: (b, 0, 0, 0)),
        scratch_shapes=[pltpu.VMEM((ATT_HEADS, HEAD_DIM, V7X_LANES), F32)],
    )
    return pl.pallas_call(
        functools.partial(_sample_select_kernel, pages_per_step=pages_per_step, n_blocks=n_blocks),
        grid_spec=grid_spec,
        out_shape=jax.ShapeDtypeStruct((db, ATT_HEADS, t_len, V7X_LANES), jnp.int32),
        compiler_params=pltpu.CompilerParams(
            dimension_semantics=("arbitrary", "arbitrary"), vmem_limit_bytes=VMEM_LIMIT_BYTES),
        name="sample_select",
    )(page_table.reshape(-1), q_s, page_table.reshape(db, 1, n_pages),
      *([cache_t] * pages_per_step))


def _sample_attn_kernel(ph_ref, qt_ref, kt_ref, vt_ref, kc_ref, vc_ref, o_ref, kbuf, vbuf, sem,
                        *, t_len, layer):
    n_sel = MOBA_TOPK * (MOBA_BLOCK // PAGE_SIZE)
    n_tiles = t_len * n_sel
    heads = pl.num_programs(1)
    step = pl.program_id(0) * heads + pl.program_id(1)
    n_steps = pl.num_programs(0) * heads
    slot = step % 2

    def tile_copies(step_i, slot_i, i):
        page = ph_ref[step_i * n_tiles + i]
        head = step_i % heads
        return (pltpu.make_async_copy(kc_ref.at[layer, page, head], kbuf.at[slot_i, i], sem.at[0, slot_i]),
                pltpu.make_async_copy(vc_ref.at[layer, page, head], vbuf.at[slot_i, i], sem.at[1, slot_i]))

    def start_all(step_i, slot_i):
        def body(i, c):
            for cp in tile_copies(step_i, slot_i, i):
                cp.start()
            return c
        lax.fori_loop(0, n_tiles, body, 0)

    @pl.when(step == 0)
    def _():
        start_all(0, 0)

    @pl.when(step + 1 < n_steps)
    def _():
        start_all(step + 1, 1 - slot)

    def wait_tile(i, c):
        for cp in tile_copies(step, slot, i):
            cp.wait()
        return c

    lax.fori_loop(0, n_tiles, wait_tile, 0)

    ktn = kt_ref[...]
    vtn = vt_ref[...]
    qtn = qt_ref[...] * ATT_SCALE
    row_sel = lax.broadcasted_iota(jnp.int32, (t_len, PAGE_SIZE), 0)
    row_own = lax.broadcasted_iota(jnp.int32, (t_len, t_len), 0)
    col_own = lax.broadcasted_iota(jnp.int32, (t_len, t_len), 1)
    s_sel = [jnp.zeros((t_len, PAGE_SIZE), F32) for _ in range(n_sel)]
    s_own = jnp.zeros((t_len, t_len), F32)
    for t in range(t_len):
        qc = qtn[:, t:t + 1]
        for i in range(n_sel):
            s = jnp.sum(kbuf[slot, t * n_sel + i] * qc, axis=0, keepdims=True)
            s_sel[i] = jnp.where(row_sel == t, s, s_sel[i])
        s_own = jnp.where(row_own == t, jnp.sum(ktn * qc, axis=0, keepdims=True), s_own)
    s_own = jnp.where(col_own <= row_own, s_own, NEG_INF)
    m_sel = s_sel[0]
    for i in range(1, n_sel):
        m_sel = jnp.maximum(m_sel, s_sel[i])
    m = jnp.maximum(jnp.max(m_sel, axis=-1, keepdims=True), jnp.max(s_own, axis=-1, keepdims=True))
    p_sel = [jnp.exp(s - m) for s in s_sel]
    p_own = jnp.exp(s_own - m)
    p_tot = p_sel[0]
    for i in range(1, n_sel):
        p_tot = p_tot + p_sel[i]
    den = jnp.sum(p_tot, axis=-1, keepdims=True) + jnp.sum(p_own, axis=-1, keepdims=True)
    for t in range(t_len):
        acc = vbuf[slot, t * n_sel] * p_sel[0][t:t + 1, :]
        for i in range(1, n_sel):
            acc = acc + vbuf[slot, t * n_sel + i] * p_sel[i][t:t + 1, :]
        o = (jnp.sum(acc, axis=-1, keepdims=True)
             + jnp.sum(vtn * p_own[t:t + 1, :], axis=-1, keepdims=True))
        o_ref[:, t:t + 1] = o / den[t:t + 1, :]


def _sample_attn(phys, qt_s, kt_s, vt_s, cache_kt, cache_vt, layer):
    db, _, t_len = qt_s.shape
    n_tiles = t_len * MOBA_TOPK * (MOBA_BLOCK // PAGE_SIZE)
    new_spec = pl.BlockSpec((None, HEAD_DIM, t_len), lambda b, h, ph: (b, h, 0))
    cache_spec = pl.BlockSpec(memory_space=pl.ANY)
    grid_spec = pltpu.PrefetchScalarGridSpec(
        num_scalar_prefetch=1,
        grid=(db, ATT_HEADS),
        in_specs=[new_spec, new_spec, new_spec, cache_spec, cache_spec],
        out_specs=new_spec,
        scratch_shapes=[
            pltpu.VMEM((2, n_tiles, HEAD_DIM, PAGE_SIZE), F32),
            pltpu.VMEM((2, n_tiles, HEAD_DIM, PAGE_SIZE), F32),
            pltpu.SemaphoreType.DMA((2, 2)),
        ],
    )
    return pl.pallas_call(
        functools.partial(_sample_attn_kernel, t_len=t_len, layer=layer),
        grid_spec=grid_spec,
        out_shape=jax.ShapeDtypeStruct((db, ATT_WIDTH, t_len), F32),
        compiler_params=pltpu.CompilerParams(
            dimension_semantics=("arbitrary", "arbitrary"), vmem_limit_bytes=VMEM_LIMIT_BYTES),
        name="sample_attn",
    )(phys.reshape(-1), qt_s, kt_s, vt_s, cache_kt, cache_vt)


def _sample_out_kernel(x_ref, att_ref, sg_ref, ml_ref, np0_ref, wo_ref, npre_ref, wi_ref,
                       lg_ref, lb_ref, cd_ref, bst_ref, wc_ref, np1_ref, y_ref, gv_ref):
    n = x_ref.shape[0]
    t_len = cd_ref.shape[1]
    reps = n // t_len
    op = _mix_out_ab(att_ref[...], sg_ref[...], ml_ref[...], wo_ref)
    y0 = x_ref[...] + _rms_norm(op, np0_ref[...])
    u, v, g = _gmlp_in(y0, npre_ref, wi_ref, lg_ref, lb_ref)
    gv_ref[...] = v
    tile = lambda tab: jnp.concatenate([tab] * reps, axis=0)
    mix = tile(bst_ref[...]) + tile(cd_ref[0]) * v
    for d in range(1, t_len):
        mix = mix + tile(cd_ref[d]) * pltpu.roll(v, d, 0)
    z = _silu(g) * (u * mix)
    op1 = _dot(z.astype(BF16), wc_ref[...])
    y_ref[...] = y0 + _rms_norm(op1, np1_ref[...])


def _sample_out(xs, att, sg, ml, np0, wo, npre1, wi, lg, lb, cd, bst8, wc, np1):
    n, d = xs.shape
    args = (xs, att, sg, ml, np0, wo, npre1, wi, lg, lb, cd, bst8, wc, np1)
    full = lambda a: pl.BlockSpec(a.shape, lambda i: (0,) * a.ndim)
    outs = (jax.ShapeDtypeStruct((n, d), F32), jax.ShapeDtypeStruct((n, GMLP_WIDTH), F32))
    return pl.pallas_call(
        _sample_out_kernel,
        grid=(1,),
        in_specs=[full(a) for a in args],
        out_specs=tuple(pl.BlockSpec(o.shape, lambda i: (0, 0)) for o in outs),
        out_shape=outs,
        compiler_params=pltpu.CompilerParams(
            dimension_semantics=("arbitrary",), vmem_limit_bytes=VMEM_LIMIT_BYTES),
        name="sample_out",
    )(*args)


def _block_diag(wh):
    h, n, _ = wh.shape
    eye = jnp.eye(h, dtype=wh.dtype)
    return jnp.einsum("hij,hg->higj", wh, eye).reshape(h * n, h * n)


def kernel(x_prompt, x_sample, cache_k, cache_v, page_table, state_lru_h, state_conv, norm_pre, norm_post, w_in_ab, conv_w, conv_b, lru_wa, lru_ba, lru_wx, lru_bx, lru_lambda, w_out_ab, w_in_c, c_ln_g, c_ln_b, c_ws, c_bs, w_out_c):
    b, s, d = x_prompt.shape
    db, t_len, _ = x_sample.shape
    n_pages = page_table.shape[1]
    assert norm_pre.shape[0] == 2 and w_in_ab.shape[0] == 1 and w_in_c.shape[0] == 1
    assert s % (2 * MOBA_BLOCK) == 0 and t_len == V7X_SUBLANES
    assert (n_pages * PAGE_SIZE) % MOBA_BLOCK == 0 and t_len <= CHUNK
    assert cache_k.shape[2:] == (PAGE_SIZE, ATT_HEADS, HEAD_DIM)
    w = LRU_WIDTH
    row = lambda vec: vec.reshape(1, -1)

    whi = w_in_ab[0].astype(BF16)
    wq = w_in_ab[0][:, :2 * ATT_WIDTH]
    wlo = (wq - wq.astype(BF16).astype(F32)).astype(BF16)
    wa = _block_diag(lru_wa[0]).astype(BF16)
    wx = _block_diag(lru_wx[0]).astype(BF16)
    lru_args = (conv_w[0], row(conv_b[0]), wa, wx, row(lru_ba[0]), row(lru_bx[0]), row(lru_lambda[0]))
    wo = w_out_ab[0].astype(BF16)
    wi = w_in_c[0].astype(BF16)
    wc = w_out_c[0].astype(BF16)
    np0, np1 = row(norm_post[0]), row(norm_post[1])
    npre0, npre1 = row(norm_pre[0]), row(norm_pre[1])
    lg, lb = row(c_ln_g[0]), row(c_ln_b[0])
    bst = jnp.repeat(c_bs[0].T, GMLP_GROUP_DIM, axis=1)

    (qt, kt, vt, vtb, kn, km, sg, ml, h_last, conv_last) = _prompt_in(
        x_prompt, npre0, whi, wlo, *lru_args, tm=2 * MOBA_BLOCK)
    att = _prompt_attn(qt, kn, vtb, km)
    y_prompt = _prompt_out(
        x_prompt.reshape(b * s, d), att.reshape(b * s, ATT_WIDTH), sg.reshape(b * s, ATT_WIDTH),
        ml.reshape(b * s, w), np0, wo, npre1, wi, lg, lb, c_ws[0], bst, wc, np1,
        tm=MOBA_BLOCK).reshape(b, s, d)
    heads_last = lambda xt: xt.reshape(b, ATT_HEADS, HEAD_DIM, s).transpose(0, 3, 1, 2)[None]
    k_prompt, v_prompt = heads_last(kt), heads_last(vt)

    n = db * t_len
    xs = x_sample.reshape(n, d)
    h0rep = jnp.repeat(state_lru_h[0], t_len, axis=0)
    stpad = jnp.pad(state_conv[0], ((0, 0), (t_len - (CONV_WIDTH - 1), 0), (0, 0))).reshape(n, w)
    q_s, k_s, v_s, sg_s, ml_s, hs_s, xl_s = _sample_in(xs, npre0, whi, wlo, *lru_args, h0rep, stpad)

    cache_kt = cache_k.transpose(0, 1, 3, 4, 2)
    cache_vt = cache_v.transpose(0, 1, 3, 4, 2)
    phys = _sample_select(page_table, q_s, cache_kt, 0, pages_per_step=32)
    tok_last = lambda a: a.reshape(db, t_len, ATT_WIDTH).transpose(0, 2, 1)
    att_t = _sample_attn(phys[..., :MOBA_TOPK * (MOBA_BLOCK // PAGE_SIZE)], tok_last(q_s),
                         tok_last(k_s), tok_last(v_s), cache_kt, cache_vt, 0)
    att_s = att_t.transpose(0, 2, 1).reshape(n, ATT_WIDTH)

    ws8 = c_ws[0][:, :t_len, :t_len]
    tt = jnp.arange(t_len)
    diag = lambda dd: jnp.where(tt >= dd, ws8[:, tt, jnp.maximum(tt - dd, 0)], 0.0)
    cd = jnp.stack([jnp.repeat(diag(dd).T, GMLP_GROUP_DIM, axis=1) for dd in range(t_len)])
    y_s, gv_s = _sample_out(xs, att_s, sg_s, ml_s, np0, wo, npre1, wi, lg, lb, cd, bst[:t_len],
                            wc, np1)

    per_req = lambda a, width: a.reshape(db, t_len, width)
    return (y_prompt, y_s.reshape(db, t_len, d), k_prompt, v_prompt,
            h_last.reshape(1, b, w), conv_last[None],
            k_s.reshape(1, db, t_len, ATT_HEADS, HEAD_DIM), v_s.reshape(1, db, t_len, ATT_HEADS, HEAD_DIM),
            per_req(hs_s, w)[:, t_len - 1][None], per_req(xl_s, w)[:, t_len - (CONV_WIDTH - 1):][None],
            per_req(gv_s, GMLP_WIDTH)[None])
```

```python
import functools

import jax
import jax.numpy as jnp
from jax import lax
from jax.experimental import pallas as pl
from jax.experimental.pallas import tpu as pltpu

ATT_HEADS = 8
HEAD_DIM = 64
ATT_WIDTH = ATT_HEADS * HEAD_DIM
HEAD_PAIRS = ATT_HEADS // 2
MOBA_BLOCK = 256
MOBA_TOPK = 3
QUERY_BLOCK = 128
LRU_WIDTH = 512
CONV_WIDTH = 4
LRU_C = 8.0
GMLP_WIDTH = 1024
GMLP_GROUPS = 8
GMLP_GROUP_DIM = GMLP_WIDTH // GMLP_GROUPS
CHUNK = 128
PAGE_SIZE = 128
NORM_EPS = 1e-6
ATT_SCALE = HEAD_DIM ** -0.5
LOG2_E = 1.4426950408889634

V7X_LANES = 128
V7X_SUBLANES = 8
VMEM_LIMIT_BYTES = 56 * 1024 * 1024

F32 = jnp.float32
BF16 = jnp.bfloat16
NEG_INF = float("-inf")
POS_INF = float("inf")


def _rms_norm(x, g):
    return x * lax.rsqrt(jnp.mean(x * x, axis=-1, keepdims=True) + NORM_EPS) * g


def _layer_norm(x, g, b):
    mu = jnp.mean(x, axis=-1, keepdims=True)
    xc = x - mu
    var = jnp.mean(xc * xc, axis=-1, keepdims=True)
    return xc * lax.rsqrt(var + NORM_EPS) * g + b


def _sigmoid(x):
    return 1.0 / (1.0 + jnp.exp(-x))


def _silu(x):
    return x * _sigmoid(x)


def _gelu_tanh(x):
    c = 0.7978845608028654
    return 0.5 * x * (1.0 + jnp.tanh(c * (x + 0.044715 * (x * x * x))))


def _softplus(x):
    return jnp.maximum(x, 0.0) + jnp.log1p(jnp.exp(-jnp.abs(x)))


def _dot(a, b):
    return jnp.dot(a, b, preferred_element_type=F32)


def _dot_exact(a, b):
    return jnp.dot(a, b, preferred_element_type=F32, precision=lax.Precision.HIGHEST)


def _split_bf16(x):
    hi = x.astype(BF16)
    lo = (x - hi.astype(F32)).astype(BF16)
    return hi, lo


def _in_proj_ab(xn, whi_ref, wlo_ref):
    xh, xl = _split_bf16(xn)
    q = (_dot(xh, whi_ref[:, :ATT_WIDTH]) + _dot(xh, wlo_ref[...])
         + _dot(xl, whi_ref[:, :ATT_WIDTH]))
    rest = _dot(xh, whi_ref[:, ATT_WIDTH:])
    return q, rest[:, :ATT_WIDTH], rest[:, ATT_WIDTH:]


def _lru_coeffs(xc, wa_ref, wx_ref, ba_ref, bx_ref, lam_ref):
    xcb = xc.astype(BF16)
    r = _sigmoid(_dot(xcb, wa_ref[...]) + ba_ref[...])
    i = _sigmoid(_dot(xcb, wx_ref[...]) + bx_ref[...])
    log_a = (-LRU_C) * r * _softplus(-lam_ref[...])
    a = jnp.exp(log_a)
    u = jnp.sqrt(-jnp.tanh(log_a) * (a * a + 1.0)) * (i * xc)
    return a, u


def _top3_rows(gate, blk, n_valid):
    nb = gate.shape[0]
    sel = jnp.zeros(gate.shape, F32)
    for r in range(MOBA_TOPK):
        mx = jnp.max(gate, axis=0, keepdims=True)
        ix = jnp.min(jnp.where(gate == mx, blk, nb), axis=0, keepdims=True)
        hit = blk == ix
        sel = jnp.maximum(sel, jnp.where(hit, jnp.where(r < n_valid, 1.0, 0.0), 0.0))
        gate = jnp.where(hit, NEG_INF, gate)
    return sel


def _prompt_in_kernel(x_ref, g_ref, whi_ref, wlo_ref, cw_ref, cb_ref, wa_ref, wx_ref,
                      ba_ref, bx_ref, lam_ref,
                      qt_ref, kt_ref, vt_ref, vtb_ref, kn_ref, xm_ref, sg_ref, ml_ref,
                      hl_ref, cl_ref,
                      xbuf, hcar, abuf, ubuf):
    t = pl.program_id(1)
    tm = x_ref.shape[0]
    w = LRU_WIDTH

    @pl.when(t == 0)
    def _():
        xbuf[0:V7X_SUBLANES, :] = jnp.zeros((V7X_SUBLANES, w), F32)
        hcar[...] = jnp.zeros(hcar.shape, F32)

    xn = _rms_norm(x_ref[...], g_ref[...])
    q, k, rest = _in_proj_ab(xn, whi_ref, wlo_ref)
    v = rest[:, :ATT_WIDTH]
    g_att = rest[:, ATT_WIDTH:2 * ATT_WIDTH]
    x_lru = rest[:, 2 * ATT_WIDTH:2 * ATT_WIDTH + w]
    g_lru = rest[:, 2 * ATT_WIDTH + w:]

    for s in range(tm // QUERY_BLOCK):
        rows = slice(s * QUERY_BLOCK, (s + 1) * QUERY_BLOCK)
        qt_ref[s] = q[rows, :].T
    kt_ref[...] = k.T
    vt = v.T
    vt_ref[...] = vt
    kn_ref[...] = k.astype(BF16)
    for s in range(tm // MOBA_BLOCK):
        vtb_ref[s] = vt[:, s * MOBA_BLOCK:(s + 1) * MOBA_BLOCK].astype(BF16)
        xm_ref[pl.ds(t * (tm // MOBA_BLOCK) + s, 1), :] = jnp.mean(
            xn[s * MOBA_BLOCK:(s + 1) * MOBA_BLOCK, :], axis=0, keepdims=True)
    sg_ref[...] = _silu(g_att).astype(BF16)

    xbuf[V7X_SUBLANES:, :] = x_lru
    xc = cw_ref[CONV_WIDTH - 1:CONV_WIDTH, :] * x_lru + cb_ref[...]
    for back in range(1, CONV_WIDTH):
        xc = xc + (cw_ref[CONV_WIDTH - 1 - back:CONV_WIDTH - back, :]
                   * xbuf[pl.ds(V7X_SUBLANES - back, tm), :])
    xbuf[0:V7X_SUBLANES, :] = x_lru[tm - V7X_SUBLANES:, :]

    a, u = _lru_coeffs(xc, wa_ref, wx_ref, ba_ref, bx_ref, lam_ref)
    abuf[...] = a
    ubuf[...] = u

    row = lax.broadcasted_iota(jnp.int32, (V7X_SUBLANES, w), 0)

    def group(gi, h):
        r0 = pl.multiple_of(gi * V7X_SUBLANES, V7X_SUBLANES)
        ag = abuf[pl.ds(r0, V7X_SUBLANES), :]
        ug = ubuf[pl.ds(r0, V7X_SUBLANES), :]
        for d in (1, 2, 4):
            keep = row >= d
            ug = jnp.where(keep, ag * pltpu.roll(ug, d, 0) + ug, ug)
            ag = jnp.where(keep, ag * pltpu.roll(ag, d, 0), ag)
        hg = ag * h + ug
        ubuf[pl.ds(r0, V7X_SUBLANES), :] = hg
        return jnp.broadcast_to(hg[V7X_SUBLANES - 1:, :], (V7X_SUBLANES, w))

    hcar[...] = lax.fori_loop(0, tm // V7X_SUBLANES, group, hcar[...])
    hs = ubuf[...]
    ml_ref[...] = (_silu(g_lru) * hs).astype(BF16)

    @pl.when(t == pl.num_programs(1) - 1)
    def _():
        hl_ref[...] = hs[tm - 1:, :]
        cl_ref[...] = x_lru[tm - (CONV_WIDTH - 1):, :]


def _prompt_in(x, g_pre, whi, wlo, cw, cb, wa, wx, ba, bx, lam, tm):
    b, s, d = x.shape
    w = LRU_WIDTH
    nb = s // MOBA_BLOCK
    nq = s // QUERY_BLOCK
    const = lambda shape: pl.BlockSpec(shape, lambda bi, ti: (0,) * len(shape))
    out_shape = (
        jax.ShapeDtypeStruct((b, nq, ATT_WIDTH, QUERY_BLOCK), F32),
        jax.ShapeDtypeStruct((b, ATT_WIDTH, s), F32),
        jax.ShapeDtypeStruct((b, ATT_WIDTH, s), F32),
        jax.ShapeDtypeStruct((b, nb, ATT_WIDTH, MOBA_BLOCK), BF16),
        jax.ShapeDtypeStruct((b, s, ATT_WIDTH), BF16),
        jax.ShapeDtypeStruct((b, nb, d), F32),
        jax.ShapeDtypeStruct((b, s, ATT_WIDTH), BF16),
        jax.ShapeDtypeStruct((b, s, w), BF16),
        jax.ShapeDtypeStruct((b, 1, w), F32),
        jax.ShapeDtypeStruct((b, CONV_WIDTH - 1, w), F32),
    )
    out_specs = (
        pl.BlockSpec((None, tm // QUERY_BLOCK, ATT_WIDTH, QUERY_BLOCK), lambda bi, ti: (bi, ti, 0, 0)),
        pl.BlockSpec((None, ATT_WIDTH, tm), lambda bi, ti: (bi, 0, ti)),
        pl.BlockSpec((None, ATT_WIDTH, tm), lambda bi, ti: (bi, 0, ti)),
        pl.BlockSpec((None, tm // MOBA_BLOCK, ATT_WIDTH, MOBA_BLOCK), lambda bi, ti: (bi, ti, 0, 0)),
        pl.BlockSpec((None, tm, ATT_WIDTH), lambda bi, ti: (bi, ti, 0)),
        pl.BlockSpec((None, nb, d), lambda bi, ti: (bi, 0, 0)),
        pl.BlockSpec((None, tm, ATT_WIDTH), lambda bi, ti: (bi, ti, 0)),
        pl.BlockSpec((None, tm, w), lambda bi, ti: (bi, ti, 0)),
        pl.BlockSpec((None, 1, w), lambda bi, ti: (bi, 0, 0)),
        pl.BlockSpec((None, CONV_WIDTH - 1, w), lambda bi, ti: (bi, 0, 0)),
    )
    in_specs = [
        pl.BlockSpec((None, tm, d), lambda bi, ti: (bi, ti, 0)),
        const((1, d)), const(whi.shape), const(wlo.shape), const(cw.shape), const(cb.shape),
        const(wa.shape), const(wx.shape), const(ba.shape), const(bx.shape), const(lam.shape),
    ]
    return pl.pallas_call(
        _prompt_in_kernel,
        grid=(b, s // tm),
        in_specs=in_specs,
        out_specs=out_specs,
        out_shape=out_shape,
        scratch_shapes=[
            pltpu.VMEM((tm + V7X_SUBLANES, w), F32),
            pltpu.VMEM((V7X_SUBLANES, w), F32),
            pltpu.VMEM((tm, w), F32),
            pltpu.VMEM((tm, w), F32),
        ],
        compiler_params=pltpu.CompilerParams(
            dimension_semantics=("arbitrary", "arbitrary"),
            vmem_limit_bytes=VMEM_LIMIT_BYTES),
        name="prompt_in",
    )(x, g_pre, whi, wlo, cw, cb, wa, wx, ba, bx, lam)


def _prompt_attn_kernel(qt_ref, kn_ref, vtb_ref, xm_ref, wk_ref, o_ref, w_ref, sel_ref, sa_ref,
                        sb_ref, acc_ref):
    nsb = vtb_ref.shape[0]
    nb = xm_ref.shape[0]
    tq = MOBA_BLOCK
    hd = HEAD_DIM
    ones_rows = 2 * V7X_SUBLANES
    first = lax.broadcasted_iota(jnp.int32, (2 * hd, tq), 0) < hd
    kmp = _dot_exact(xm_ref[...], wk_ref[...])
    blk = lax.broadcasted_iota(jnp.int32, (nb, tq), 0)
    ones_tile = jnp.ones((ones_rows, tq), BF16)
    causal = (lax.broadcasted_iota(jnp.int32, (tq, tq), 0)
              <= lax.broadcasted_iota(jnp.int32, (tq, tq), 1))

    def stage(sb, c):
        qt = jnp.concatenate([qt_ref[2 * sb], qt_ref[2 * sb + 1]], axis=1)
        for h in range(2):
            qth = jnp.where(first, qt, 0.0) if h == 0 else jnp.where(first, 0.0, qt)
            gate = _dot_exact(kmp, qth)
            gate = jnp.where(blk < sb, gate, NEG_INF)
            sel_ref[sb, h] = _top3_rows(gate, blk, sb)
            w_ref[sb, :, h * tq:(h + 1) * tq] = (qth * (ATT_SCALE * LOG2_E)).astype(BF16)
        return c

    lax.fori_loop(0, nsb, stage, 0, unroll=2)

    def v_lhs(j, h):
        return jnp.concatenate([vtb_ref[j, h * hd:(h + 1) * hd, :], ones_tile], axis=0)

    def key_block(j):
        return kn_ref[pl.ds(pl.multiple_of(j * MOBA_BLOCK, MOBA_BLOCK), MOBA_BLOCK), :]

    heads = (0, 1)
    bufs = (sa_ref, sb_ref)
    hcols = tuple(slice(h * tq, (h + 1) * tq) for h in heads)

    def scores_into(buf, j, wq):
        sc = _dot(key_block(j), wq)
        buf[...] = sc
        return tuple(jnp.max(sc[:, hcols[h]], axis=0, keepdims=True) for h in heads)

    def super_block(sb, cm_first):
        last = sb - 1
        wq = w_ref[sb]
        for h in heads:
            acc_ref[h] = jnp.zeros(acc_ref.shape[1:], F32)
        m_init = tuple(jnp.full((1, tq), NEG_INF, F32) for _ in heads)

        def past_block(j, live, cur, nxt, carry):
            m_old, cm = carry
            cm_next = scores_into(bufs[nxt], jnp.minimum(j + 1, sb), wq)
            m_out = []
            for h in heads:
                on = sel_ref[sb, h, pl.ds(j, 1), :] * live > 0.0
                m_new = jnp.maximum(m_old[h], jnp.where(on, cm[h], NEG_INF))
                p = jnp.exp2(bufs[cur][:, hcols[h]] - jnp.where(on, m_new, POS_INF)).astype(BF16)
                alpha = jnp.exp2(m_old[h] - jnp.where(m_new == NEG_INF, 0.0, m_new))
                acc_ref[h] = acc_ref[h] * alpha + _dot(v_lhs(j, h), p)
                m_out.append(m_new)
            return tuple(m_out), cm_next

        def past_pair(j0, carry):
            j1 = jnp.minimum(j0 + 1, last)
            mid = past_block(j0, 1.0, 1, 0, carry)
            return past_block(j1, jnp.where(j0 + 1 < sb, 1.0, 0.0), 0, 1, mid)

        def past_run(j0, n_pairs, cr):
            for k in range(n_pairs):
                cr = past_pair(j0 + 2 * k, cr)
            return cr

        n_long = sb // 8
        carry = lax.fori_loop(0, n_long, lambda i, cr: past_run(8 * i, 4, cr), (m_init, cm_first))
        done = 8 * n_long
        n_mid = (sb - done) // 4
        carry = lax.fori_loop(0, n_mid, lambda i, cr: past_run(done + 4 * i, 2, cr), carry)
        done = done + 4 * n_mid
        m_past, _ = lax.fori_loop(0, (sb - done + 1) // 2,
                                  lambda i, cr: past_pair(done + 2 * i, cr), carry)

        accs = []
        for h in heads:
            s = jnp.where(causal, bufs[1][:, hcols[h]], NEG_INF)
            m_new = jnp.maximum(m_past[h], jnp.max(s, axis=0, keepdims=True))
            p = jnp.exp2(s - m_new).astype(BF16)
            accs.append(acc_ref[h] * jnp.exp2(m_past[h] - m_new) + _dot(v_lhs(sb, h), p))
        cm_next = scores_into(bufs[1], 0, w_ref[jnp.minimum(sb + 1, nsb - 1)])
        ot = jnp.concatenate([a[:hd, :] / a[hd:hd + 1, :] for a in accs], axis=0)
        o_ref[pl.ds(pl.multiple_of(sb * tq, tq), tq), :] = ot.T
        return cm_next

    lax.fori_loop(0, nsb, super_block, scores_into(bufs[1], 0, w_ref[0]))


def _prompt_attn(qt, kn, vtb, xm, w_in):
    b, nq, _, qb = qt.shape
    s = kn.shape[1]
    nb, d = xm.shape[1:]
    pw = 2 * HEAD_DIM
    tq = MOBA_BLOCK
    k_col0 = ATT_WIDTH // pw
    return pl.pallas_call(
        _prompt_attn_kernel,
        grid=(b, HEAD_PAIRS),
        in_specs=[
            pl.BlockSpec((None, nq, pw, qb), lambda bi, pi: (bi, 0, pi, 0)),
            pl.BlockSpec((None, s, pw), lambda bi, pi: (bi, 0, pi)),
            pl.BlockSpec((None, nb, pw, tq), lambda bi, pi: (bi, 0, pi, 0)),
            pl.BlockSpec((None, nb, d), lambda bi, pi: (bi, 0, 0)),
            pl.BlockSpec((d, pw), lambda bi, pi: (0, k_col0 + pi)),
        ],
        out_specs=pl.BlockSpec((None, s, pw), lambda bi, pi: (bi, 0, pi)),
        out_shape=jax.ShapeDtypeStruct((b, s, ATT_WIDTH), F32),
        scratch_shapes=[
            pltpu.VMEM((nb, pw, 2 * tq), BF16),
            pltpu.VMEM((nb, 2, nb, tq), F32),
            pltpu.VMEM((tq, 2 * tq), F32), pltpu.VMEM((tq, 2 * tq), F32),
            pltpu.VMEM((2, HEAD_DIM + 2 * V7X_SUBLANES, tq), F32),
        ],
        compiler_params=pltpu.CompilerParams(
            dimension_semantics=("arbitrary", "arbitrary"),
            vmem_limit_bytes=VMEM_LIMIT_BYTES),
        name="prompt_attn",
    )(qt, kn, vtb, xm, w_in)


def _mix_out_ab(att, sg, ml, wo_ref):
    ma = (sg.astype(F32) * att).astype(BF16)
    return _dot(ma, wo_ref[:ATT_WIDTH, :]) + _dot(ml.astype(BF16), wo_ref[ATT_WIDTH:, :])


def _gmlp_in(y0, npre_ref, wi_ref, lg_ref, lb_ref):
    xn = _rms_norm(y0, npre_ref[...])
    pr = _dot(xn.astype(BF16), wi_ref[...])
    u = _gelu_tanh(pr[:, :GMLP_WIDTH])
    v = _layer_norm(_gelu_tanh(pr[:, GMLP_WIDTH:2 * GMLP_WIDTH]), lg_ref[...], lb_ref[...])
    g = pr[:, 2 * GMLP_WIDTH:]
    return u, v, g


def _prompt_out_kernel(x_ref, att_ref, sg_ref, ml_ref, np0_ref, wo_ref, npre_ref, wi_ref,
                       lg_ref, lb_ref, ws_ref, bst_ref, wc_ref, np1_ref, y_ref, mix_ref):
    tm = x_ref.shape[0]
    op = _mix_out_ab(att_ref[...], sg_ref[...], ml_ref[...], wo_ref)
    y0 = x_ref[...] + _rms_norm(op, np0_ref[...])
    u, v, g = _gmlp_in(y0, npre_ref, wi_ref, lg_ref, lb_ref)
    vb = v.astype(BF16)
    t_out = lax.broadcasted_iota(jnp.int32, (CHUNK, CHUNK), 0)
    t_in = lax.broadcasted_iota(jnp.int32, (CHUNK, CHUNK), 1)
    for gi in range(GMLP_GROUPS):
        wm = jnp.where(t_in <= t_out, ws_ref[gi], 0.0).astype(BF16)
        cols = slice(gi * GMLP_GROUP_DIM, (gi + 1) * GMLP_GROUP_DIM)
        for c in range(tm // CHUNK):
            rows = slice(c * CHUNK, (c + 1) * CHUNK)
            mix_ref[rows, cols] = _dot(wm, vb[rows, cols]) + bst_ref[:, cols]
    z = _silu(g) * (u * mix_ref[...])
    op1 = _dot(z.astype(BF16), wc_ref[...])
    y_ref[...] = y0 + _rms_norm(op1, np1_ref[...])


def _prompt_out(x2, att2, sg2, ml2, np0, wo, npre1, wi, lg, lb, ws, bst, wc, np1, tm):
    n, d = x2.shape
    const = lambda shape: pl.BlockSpec(shape, lambda i: (0,) * len(shape))
    rows = lambda width: pl.BlockSpec((tm, width), lambda i: (i, 0))
    return pl.pallas_call(
        _prompt_out_kernel,
        grid=(n // tm,),
        in_specs=[rows(d), rows(ATT_WIDTH), rows(ATT_WIDTH), rows(LRU_WIDTH),
                  const(np0.shape), const(wo.shape), const(npre1.shape), const(wi.shape),
                  const(lg.shape), const(lb.shape), const(ws.shape), const(bst.shape),
                  const(wc.shape), const(np1.shape)],
        out_specs=rows(d),
        out_shape=jax.ShapeDtypeStruct((n, d), F32),
        scratch_shapes=[pltpu.VMEM((tm, GMLP_WIDTH), F32)],
        compiler_params=pltpu.CompilerParams(
            dimension_semantics=("arbitrary",),
            vmem_limit_bytes=VMEM_LIMIT_BYTES),
        name="prompt_out",
    )(x2, att2, sg2, ml2, np0, wo, npre1, wi, lg, lb, ws, bst, wc, np1)


def _sample_in_kernel(x_ref, g_ref, whi_ref, wlo_ref, cw_ref, cb_ref, wa_ref, wx_ref,
                      ba_ref, bx_ref, lam_ref, h0_ref, st_ref,
                      q_ref, k_ref, v_ref, sg_ref, ml_ref, hs_ref, xl_ref):
    n = x_ref.shape[0]
    t_len = V7X_SUBLANES
    w = LRU_WIDTH
    xn = _rms_norm(x_ref[...], g_ref[...])
    q, k, rest = _in_proj_ab(xn, whi_ref, wlo_ref)
    q_ref[...] = q
    k_ref[...] = k
    v_ref[...] = rest[:, :ATT_WIDTH]
    g_att = rest[:, ATT_WIDTH:2 * ATT_WIDTH]
    x_lru = rest[:, 2 * ATT_WIDTH:2 * ATT_WIDTH + w]
    g_lru = rest[:, 2 * ATT_WIDTH + w:]
    sg_ref[...] = _silu(g_att)
    xl_ref[...] = x_lru

    tok = lax.broadcasted_iota(jnp.int32, (n, w), 0) % t_len
    st = st_ref[...]
    xc = cw_ref[CONV_WIDTH - 1:CONV_WIDTH, :] * x_lru + cb_ref[...]
    for back in range(1, CONV_WIDTH):
        prev = jnp.where(tok >= back, pltpu.roll(x_lru, back, 0),
                         pltpu.roll(st, n - t_len + back, 0))
        xc = xc + cw_ref[CONV_WIDTH - 1 - back:CONV_WIDTH - back, :] * prev

    a, u = _lru_coeffs(xc, wa_ref, wx_ref, ba_ref, bx_ref, lam_ref)
    for d in (1, 2, 4):
        keep = tok >= d
        u = jnp.where(keep, a * pltpu.roll(u, d, 0) + u, u)
        a = jnp.where(keep, a * pltpu.roll(a, d, 0), a)
    hs = a * h0_ref[...] + u
    hs_ref[...] = hs
    ml_ref[...] = _silu(g_lru) * hs


def _sample_in(xs, g_pre, whi, wlo, cw, cb, wa, wx, ba, bx, lam, h0rep, stpad):
    n, d = xs.shape
    w = LRU_WIDTH
    args = (xs, g_pre, whi, wlo, cw, cb, wa, wx, ba, bx, lam, h0rep, stpad)
    full = lambda a: pl.BlockSpec(a.shape, lambda i: (0,) * a.ndim)
    outs = [jax.ShapeDtypeStruct((n, ATT_WIDTH), F32)] * 4 + [jax.ShapeDtypeStruct((n, w), F32)] * 3
    return pl.pallas_call(
        _sample_in_kernel,
        grid=(1,),
        in_specs=[full(a) for a in args],
        out_specs=tuple(pl.BlockSpec(o.shape, lambda i: (0, 0)) for o in outs),
        out_shape=tuple(outs),
        compiler_params=pltpu.CompilerParams(
            dimension_semantics=("arbitrary",), vmem_limit_bytes=VMEM_LIMIT_BYTES),
        name="sample_in",
    )(*args)


def _sample_select_kernel(pt_ref, q_ref, ptv_ref, *refs, pages_per_step, n_blocks):
    k_refs = refs[:pages_per_step]
    idx_ref = refs[pages_per_step]
    kmt = refs[pages_per_step + 1]
    c = pl.program_id(1)
    bp = MOBA_BLOCK // PAGE_SIZE
    lane3 = lax.broadcasted_iota(jnp.int32, kmt.shape, 2)

    @pl.when(c == 0)
    def _():
        kmt[...] = jnp.zeros(kmt.shape, F32)

    for i in range(pages_per_step // bp):
        tot = k_refs[bp * i][...]
        for pg in range(1, bp):
            tot = tot + k_refs[bp * i + pg][...]
        col = jnp.sum(tot, axis=-1, keepdims=True) * (1.0 / MOBA_BLOCK)
        blk = c * (pages_per_step // bp) + i
        kmt[...] = jnp.where(lane3 == blk, col, kmt[...])

    @pl.when(c == pl.num_programs(1) - 1)
    def _():
        qv = q_ref[...]
        t_len = qv.shape[0]
        lane = lax.broadcasted_iota(jnp.int32, (t_len, V7X_LANES), 1)
        lane_pair = lax.broadcasted_iota(jnp.int32, (t_len, 2 * HEAD_DIM), 1)
        pages = jnp.broadcast_to(ptv_ref[...].astype(F32), (t_len, V7X_LANES))
        for p in range(HEAD_PAIRS):
            qp = qv[:, p * 2 * HEAD_DIM:(p + 1) * 2 * HEAD_DIM]
            kmp = kmt[2 * p:2 * p + 2].reshape(2 * HEAD_DIM, V7X_LANES)
            for hh in range(2):
                qm = jnp.where((lane_pair < HEAD_DIM) == (hh == 0), qp, 0.0)
                gate = _dot_exact(qm, kmp)
                gate = jnp.where(lane < n_blocks, gate, NEG_INF)
                out = jnp.zeros((t_len, V7X_LANES), jnp.int32)
                for r in range(MOBA_TOPK):
                    mx = jnp.max(gate, axis=-1, keepdims=True)
                    ix = jnp.min(jnp.where(gate == mx, lane, V7X_LANES), axis=-1, keepdims=True)
                    gate = jnp.where(lane == ix, NEG_INF, gate)
                    for pg in range(bp):
                        phys = jnp.sum(jnp.where(lane == ix * bp + pg, pages, 0.0),
                                       axis=-1, keepdims=True)
                        out = jnp.where(lane == r * bp + pg, phys.astype(jnp.int32), out)
                idx_ref[2 * p + hh] = out


def _sample_select(page_table, q_s, cache_t, layer, pages_per_step):
    db, n_pages = page_table.shape
    t_len = q_s.shape[0] // db
    n_blocks = n_pages * PAGE_SIZE // MOBA_BLOCK
    assert n_pages == V7X_LANES and n_pages % pages_per_step == 0

    def page_spec(i):
        return pl.BlockSpec(
            (None, None, ATT_HEADS, HEAD_DIM, PAGE_SIZE),
            lambda b, c, pt: (layer, pt[b * n_pages + c * pages_per_step + i], 0, 0, 0))

    grid_spec = pltpu.PrefetchScalarGridSpec(
        num_scalar_prefetch=1,
        grid=(db, n_pages // pages_per_step),
        in_specs=[pl.BlockSpec((t_len, ATT_WIDTH), lambda b, c, pt: (b, 0)),
                  pl.BlockSpec((None, 1, n_pages), lambda b, c, pt: (b, 0, 0))]
                 + [page_spec(i) for i in range(pages_per_step)],
        out_specs=pl.BlockSpec((None, ATT_HEADS, t_len, V7X_LANES), lambda b, c, pt: (b, 0, 0, 0)),
        scratch_shapes=[pltpu.VMEM((ATT_HEADS, HEAD_DIM, V7X_LANES), F32)],
    )
    return pl.pallas_call(
        functools.partial(_sample_select_kernel, pages_per_step=pages_per_step, n_blocks=n_blocks),
        grid_spec=grid_spec,
        out_shape=jax.ShapeDtypeStruct((db, ATT_HEADS, t_len, V7X_LANES), jnp.int32),
        compiler_params=pltpu.CompilerParams(
            dimension_semantics=("arbitrary", "arbitrary"), vmem_limit_bytes=VMEM_LIMIT_BYTES),
        name="sample_select",
    )(page_table.reshape(-1), q_s, page_table.reshape(db, 1, n_pages),
      *([cache_t] * pages_per_step))


def _sample_attn_kernel(ph_ref, qt_ref, kt_ref, vt_ref, kc_ref, vc_ref, o_ref, kbuf, vbuf, sem,
                        *, t_len, layer):
    n_sel = MOBA_TOPK * (MOBA_BLOCK // PAGE_SIZE)
    n_tiles = t_len * n_sel
    heads = pl.num_programs(1)
    step = pl.program_id(0) * heads + pl.program_id(1)
    n_steps = pl.num_programs(0) * heads
    slot = step % 2

    def tile_copies(step_i, slot_i, i):
        page = ph_ref[step_i * n_tiles + i]
        head = step_i % heads
        return (pltpu.make_async_copy(kc_ref.at[layer, page, head], kbuf.at[slot_i, i], sem.at[0, slot_i]),
                pltpu.make_async_copy(vc_ref.at[layer, page, head], vbuf.at[slot_i, i], sem.at[1, slot_i]))

    def start_all(step_i, slot_i):
        def body(i, c):
            for cp in tile_copies(step_i, slot_i, i):
                cp.start()
            return c
        lax.fori_loop(0, n_tiles, body, 0, unroll=4)

    @pl.when(step == 0)
    def _():
        start_all(0, 0)

    @pl.when(step + 1 < n_steps)
    def _():
        start_all(step + 1, 1 - slot)

    def wait_tile(i, c):
        for cp in tile_copies(step, slot, i):
            cp.wait()
        return c

    lax.fori_loop(0, n_tiles, wait_tile, 0, unroll=4)

    ktn = kt_ref[...]
    vtn = vt_ref[...]
    qtn = qt_ref[...] * ATT_SCALE
    row_sel = lax.broadcasted_iota(jnp.int32, (t_len, PAGE_SIZE), 0)
    row_own = lax.broadcasted_iota(jnp.int32, (t_len, t_len), 0)
    col_own = lax.broadcasted_iota(jnp.int32, (t_len, t_len), 1)
    s_sel = [jnp.zeros((t_len, PAGE_SIZE), F32) for _ in range(n_sel)]
    s_own = jnp.zeros((t_len, t_len), F32)
    for t in range(t_len):
        qc = qtn[:, t:t + 1]
        for i in range(n_sel):
            s = jnp.sum(kbuf[slot, t * n_sel + i] * qc, axis=0, keepdims=True)
            s_sel[i] = jnp.where(row_sel == t, s, s_sel[i])
        s_own = jnp.where(row_own == t, jnp.sum(ktn * qc, axis=0, keepdims=True), s_own)
    s_own = jnp.where(col_own <= row_own, s_own, NEG_INF)
    m_sel = s_sel[0]
    for i in range(1, n_sel):
        m_sel = jnp.maximum(m_sel, s_sel[i])
    m = jnp.maximum(jnp.max(m_sel, axis=-1, keepdims=True), jnp.max(s_own, axis=-1, keepdims=True))
    p_sel = [jnp.exp(s - m) for s in s_sel]
    p_own = jnp.exp(s_own - m)
    p_tot = p_sel[0]
    for i in range(1, n_sel):
        p_tot = p_tot + p_sel[i]
    den = jnp.sum(p_tot, axis=-1, keepdims=True) + jnp.sum(p_own, axis=-1, keepdims=True)
    for t in range(t_len):
        acc = vbuf[slot, t * n_sel] * p_sel[0][t:t + 1, :]
        for i in range(1, n_sel):
            acc = acc + vbuf[slot, t * n_sel + i] * p_sel[i][t:t + 1, :]
        o = (jnp.sum(acc, axis=-1, keepdims=True)
             + jnp.sum(vtn * p_own[t:t + 1, :], axis=-1, keepdims=True))
        o_ref[:, t:t + 1] = o / den[t:t + 1, :]


def _sample_attn(phys, qt_s, kt_s, vt_s, cache_kt, cache_vt, layer):
    db, _, t_len = qt_s.shape
    n_tiles = t_len * MOBA_TOPK * (MOBA_BLOCK // PAGE_SIZE)
    new_spec = pl.BlockSpec((None, HEAD_DIM, t_len), lambda b, h, ph: (b, h, 0))
    cache_spec = pl.BlockSpec(memory_space=pl.ANY)
    grid_spec = pltpu.PrefetchScalarGridSpec(
        num_scalar_prefetch=1,
        grid=(db, ATT_HEADS),
        in_specs=[new_spec, new_spec, new_spec, cache_spec, cache_spec],
        out_specs=new_spec,
        scratch_shapes=[
            pltpu.VMEM((2, n_tiles, HEAD_DIM, PAGE_SIZE), F32),
            pltpu.VMEM((2, n_tiles, HEAD_DIM, PAGE_SIZE), F32),
            pltpu.SemaphoreType.DMA((2, 2)),
        ],
    )
    return pl.pallas_call(
        functools.partial(_sample_attn_kernel, t_len=t_len, layer=layer),
        grid_spec=grid_spec,
        out_shape=jax.ShapeDtypeStruct((db, ATT_WIDTH, t_len), F32),
        compiler_params=pltpu.CompilerParams(
            dimension_semantics=("arbitrary", "arbitrary"), vmem_limit_bytes=VMEM_LIMIT_BYTES),
        name="sample_attn",
    )(phys.reshape(-1), qt_s, kt_s, vt_s, cache_kt, cache_vt)


def _sample_out_kernel(x_ref, att_ref, sg_ref, ml_ref, np0_ref, wo_ref, npre_ref, wi_ref,
                       lg_ref, lb_ref, cd_ref, bst_ref, wc_ref, np1_ref, y_ref, gv_ref):
    n = x_ref.shape[0]
    t_len = cd_ref.shape[1]
    reps = n // t_len
    op = _mix_out_ab(att_ref[...], sg_ref[...], ml_ref[...], wo_ref)
    y0 = x_ref[...] + _rms_norm(op, np0_ref[...])
    u, v, g = _gmlp_in(y0, npre_ref, wi_ref, lg_ref, lb_ref)
    gv_ref[...] = v
    tile = lambda tab: jnp.concatenate([tab] * reps, axis=0)
    mix = tile(bst_ref[...]) + tile(cd_ref[0]) * v
    for d in range(1, t_len):
        mix = mix + tile(cd_ref[d]) * pltpu.roll(v, d, 0)
    z = _silu(g) * (u * mix)
    op1 = _dot(z.astype(BF16), wc_ref[...])
    y_ref[...] = y0 + _rms_norm(op1, np1_ref[...])


def _sample_out(xs, att, sg, ml, np0, wo, npre1, wi, lg, lb, cd, bst8, wc, np1):
    n, d = xs.shape
    args = (xs, att, sg, ml, np0, wo, npre1, wi, lg, lb, cd, bst8, wc, np1)
    full = lambda a: pl.BlockSpec(a.shape, lambda i: (0,) * a.ndim)
    outs = (jax.ShapeDtypeStruct((n, d), F32), jax.ShapeDtypeStruct((n, GMLP_WIDTH), F32))
    return pl.pallas_call(
        _sample_out_kernel,
        grid=(1,),
        in_specs=[full(a) for a in args],
        out_specs=tuple(pl.BlockSpec(o.shape, lambda i: (0, 0)) for o in outs),
        out_shape=outs,
        compiler_params=pltpu.CompilerParams(
            dimension_semantics=("arbitrary",), vmem_limit_bytes=VMEM_LIMIT_BYTES),
        name="sample_out",
    )(*args)


def _block_diag(wh):
    h, n, _ = wh.shape
    eye = jnp.eye(h, dtype=wh.dtype)
    return jnp.einsum("hij,hg->higj", wh, eye).reshape(h * n, h * n)


def kernel(x_prompt, x_sample, cache_k, cache_v, page_table, state_lru_h, state_conv, norm_pre, norm_post, w_in_ab, conv_w, conv_b, lru_wa, lru_ba, lru_wx, lru_bx, lru_lambda, w_out_ab, w_in_c, c_ln_g, c_ln_b, c_ws, c_bs, w_out_c):
    b, s, d = x_prompt.shape
    db, t_len, _ = x_sample.shape
    n_pages = page_table.shape[1]
    assert norm_pre.shape[0] == 2 and w_in_ab.shape[0] == 1 and w_in_c.shape[0] == 1
    assert s % (2 * MOBA_BLOCK) == 0 and t_len == V7X_SUBLANES
    assert (n_pages * PAGE_SIZE) % MOBA_BLOCK == 0 and t_len <= CHUNK
    assert cache_k.shape[2:] == (PAGE_SIZE, ATT_HEADS, HEAD_DIM)
    w = LRU_WIDTH
    row = lambda vec: vec.reshape(1, -1)

    whi = w_in_ab[0].astype(BF16)
    wq = w_in_ab[0][:, :ATT_WIDTH]
    wlo = (wq - wq.astype(BF16).astype(F32)).astype(BF16)
    wa = _block_diag(lru_wa[0]).astype(BF16)
    wx = _block_diag(lru_wx[0]).astype(BF16)
    lru_args = (conv_w[0], row(conv_b[0]), wa, wx, row(lru_ba[0]), row(lru_bx[0]), row(lru_lambda[0]))
    wo = w_out_ab[0].astype(BF16)
    wi = w_in_c[0].astype(BF16)
    wc = w_out_c[0].astype(BF16)
    np0, np1 = row(norm_post[0]), row(norm_post[1])
    npre0, npre1 = row(norm_pre[0]), row(norm_pre[1])
    lg, lb = row(c_ln_g[0]), row(c_ln_b[0])
    bst = jnp.repeat(c_bs[0].T, GMLP_GROUP_DIM, axis=1)

    (qt, kt, vt, vtb, kn, xm, sg, ml, h_last, conv_last) = _prompt_in(
        x_prompt, npre0, whi, wlo, *lru_args, tm=2 * MOBA_BLOCK)
    att = _prompt_attn(qt, kn, vtb, xm, w_in_ab[0])
    y_prompt = _prompt_out(
        x_prompt.reshape(b * s, d), att.reshape(b * s, ATT_WIDTH), sg.reshape(b * s, ATT_WIDTH),
        ml.reshape(b * s, w), np0, wo, npre1, wi, lg, lb, c_ws[0], bst, wc, np1,
        tm=MOBA_BLOCK).reshape(b, s, d)
    heads_last = lambda xt: xt.reshape(b, ATT_HEADS, HEAD_DIM, s).transpose(0, 3, 1, 2)[None]
    k_prompt, v_prompt = heads_last(kt), heads_last(vt)

    n = db * t_len
    xs = x_sample.reshape(n, d)
    h0rep = jnp.repeat(state_lru_h[0], t_len, axis=0)
    stpad = jnp.pad(state_conv[0], ((0, 0), (t_len - (CONV_WIDTH - 1), 0), (0, 0))).reshape(n, w)
    q_s, k_s, v_s, sg_s, ml_s, hs_s, xl_s = _sample_in(xs, npre0, whi, wlo, *lru_args, h0rep, stpad)

    cache_kt = cache_k.transpose(0, 1, 3, 4, 2)
    cache_vt = cache_v.transpose(0, 1, 3, 4, 2)
    phys = _sample_select(page_table, q_s, cache_kt, 0, pages_per_step=32)
    tok_last = lambda a: a.reshape(db, t_len, ATT_WIDTH).transpose(0, 2, 1)
    att_t = _sample_attn(phys[..., :MOBA_TOPK * (MOBA_BLOCK // PAGE_SIZE)], tok_last(q_s),
                         tok_last(k_s), tok_last(v_s), cache_kt, cache_vt, 0)
    att_s = att_t.transpose(0, 2, 1).reshape(n, ATT_WIDTH)

    ws8 = c_ws[0][:, :t_len, :t_len]
    tt = jnp.arange(t_len)
    diag = lambda dd: jnp.where(tt >= dd, ws8[:, tt, jnp.maximum(tt - dd, 0)], 0.0)
    cd = jnp.stack([jnp.repeat(diag(dd).T, GMLP_GROUP_DIM, axis=1) for dd in range(t_len)])
    y_s, gv_s = _sample_out(xs, att_s, sg_s, ml_s, np0, wo, npre1, wi, lg, lb, cd, bst[:t_len],
                            wc, np1)

    per_req = lambda a, width: a.reshape(db, t_len, width)
    return (y_prompt, y_s.reshape(db, t_len, d), k_prompt, v_prompt,
            h_last.reshape(1, b, w), conv_last[None],
            k_s.reshape(1, db, t_len, ATT_HEADS, HEAD_DIM), v_s.reshape(1, db, t_len, ATT_HEADS, HEAD_DIM),
            per_req(hs_s, w)[:, t_len - 1][None], per_req(xl_s, w)[:, t_len - (CONV_WIDTH - 1):][None],
            per_req(gv_s, GMLP_WIDTH)[None])
```

```python
import functools

import jax
import jax.numpy as jnp
from jax import lax
from jax.experimental import pallas as pl
from jax.experimental.pallas import tpu as pltpu

ATT_HEADS = 8
HEAD_DIM = 64
ATT_WIDTH = ATT_HEADS * HEAD_DIM
HEAD_PAIRS = ATT_HEADS // 2
MOBA_BLOCK = 256
MOBA_TOPK = 3
QUERY_BLOCK = 128
LRU_WIDTH = 512
CONV_WIDTH = 4
LRU_C = 8.0
GMLP_WIDTH = 1024
GMLP_GROUPS = 8
GMLP_GROUP_DIM = GMLP_WIDTH // GMLP_GROUPS
CHUNK = 128
PAGE_SIZE = 128
NORM_EPS = 1e-6
ATT_SCALE = HEAD_DIM ** -0.5
LOG2_E = 1.4426950408889634

ATTN_TRIP_BLOCKS = (8, 4)
V7X_LANES = 128
V7X_SUBLANES = 8
VMEM_LIMIT_BYTES = 56 * 1024 * 1024

F32 = jnp.float32
BF16 = jnp.bfloat16
NEG_INF = float("-inf")
POS_INF = float("inf")


def _rms_norm(x, g):
    return x * lax.rsqrt(jnp.mean(x * x, axis=-1, keepdims=True) + NORM_EPS) * g


def _layer_norm(x, g, b):
    mu = jnp.mean(x, axis=-1, keepdims=True)
    xc = x - mu
    var = jnp.mean(xc * xc, axis=-1, keepdims=True)
    return xc * lax.rsqrt(var + NORM_EPS) * g + b


def _sigmoid(x):
    return 1.0 / (1.0 + jnp.exp(-x))


def _silu(x):
    return x * _sigmoid(x)


def _gelu_tanh(x):
    c = 0.7978845608028654
    return 0.5 * x * (1.0 + jnp.tanh(c * (x + 0.044715 * (x * x * x))))


def _softplus(x):
    return jnp.maximum(x, 0.0) + jnp.log1p(jnp.exp(-jnp.abs(x)))


def _dot(a, b):
    return jnp.dot(a, b, preferred_element_type=F32)


def _dot_exact(a, b):
    return jnp.dot(a, b, preferred_element_type=F32, precision=lax.Precision.HIGHEST)


def _split_bf16(x):
    hi = x.astype(BF16)
    lo = (x - hi.astype(F32)).astype(BF16)
    return hi, lo


def _in_proj_ab(xn, whi_ref, wlo_ref):
    xh, xl = _split_bf16(xn)
    q = (_dot(xh, whi_ref[:, :ATT_WIDTH]) + _dot(xh, wlo_ref[...])
         + _dot(xl, whi_ref[:, :ATT_WIDTH]))
    rest = _dot(xh, whi_ref[:, ATT_WIDTH:])
    return q, rest[:, :ATT_WIDTH], rest[:, ATT_WIDTH:]


def _lru_coeffs(xc, wg_ref, ba_ref, bx_ref, lam_ref):
    xcb = xc.astype(BF16)
    lanes = V7X_LANES
    pre = [_dot(xcb[:, g * lanes:(g + 1) * lanes], wg_ref[g]) for g in range(wg_ref.shape[0])]
    r = _sigmoid(jnp.concatenate([p[:, :lanes] for p in pre], axis=1) + ba_ref[...])
    i = _sigmoid(jnp.concatenate([p[:, lanes:] for p in pre], axis=1) + bx_ref[...])
    log_a = (-LRU_C) * r * _softplus(-lam_ref[...])
    a = jnp.exp(log_a)
    u = jnp.sqrt(-jnp.tanh(log_a) * (a * a + 1.0)) * (i * xc)
    return a, u


def _top3_rows(gate, blk, n_valid):
    nb = gate.shape[0]
    sel = jnp.zeros(gate.shape, F32)
    for r in range(MOBA_TOPK):
        mx = jnp.max(gate, axis=0, keepdims=True)
        ix = jnp.min(jnp.where(gate == mx, blk, nb), axis=0, keepdims=True)
        hit = blk == ix
        sel = jnp.maximum(sel, jnp.where(hit, jnp.where(r < n_valid, 1.0, 0.0), 0.0))
        gate = jnp.where(hit, NEG_INF, gate)
    return sel


def _prompt_in_kernel(x_ref, g_ref, whi_ref, wlo_ref, cw_ref, cb_ref, wg_ref,
                      ba_ref, bx_ref, lam_ref,
                      qt_ref, kt_ref, vt_ref, vtb_ref, kn_ref, xm_ref, sg_ref, ml_ref,
                      hl_ref, cl_ref,
                      xbuf, hcar, abuf, ubuf):
    t = pl.program_id(1)
    tm = x_ref.shape[0]
    w = LRU_WIDTH

    @pl.when(t == 0)
    def _():
        xbuf[0:V7X_SUBLANES, :] = jnp.zeros((V7X_SUBLANES, w), F32)
        hcar[...] = jnp.zeros(hcar.shape, F32)

    xn = _rms_norm(x_ref[...], g_ref[...])
    q, k, rest = _in_proj_ab(xn, whi_ref, wlo_ref)
    v = rest[:, :ATT_WIDTH]
    g_att = rest[:, ATT_WIDTH:2 * ATT_WIDTH]
    x_lru = rest[:, 2 * ATT_WIDTH:2 * ATT_WIDTH + w]
    g_lru = rest[:, 2 * ATT_WIDTH + w:]

    for s in range(tm // QUERY_BLOCK):
        rows = slice(s * QUERY_BLOCK, (s + 1) * QUERY_BLOCK)
        qt_ref[s] = q[rows, :].T
    kt_ref[...] = k.T
    vt = v.T
    vt_ref[...] = vt
    kn_ref[...] = k.astype(BF16)
    for s in range(tm // MOBA_BLOCK):
        vtb_ref[s] = vt[:, s * MOBA_BLOCK:(s + 1) * MOBA_BLOCK].astype(BF16)
        xm_ref[pl.ds(t * (tm // MOBA_BLOCK) + s, 1), :] = jnp.mean(
            xn[s * MOBA_BLOCK:(s + 1) * MOBA_BLOCK, :], axis=0, keepdims=True)
    sg_ref[...] = _silu(g_att).astype(BF16)

    xbuf[V7X_SUBLANES:, :] = x_lru
    xc = cw_ref[CONV_WIDTH - 1:CONV_WIDTH, :] * x_lru + cb_ref[...]
    for back in range(1, CONV_WIDTH):
        xc = xc + (cw_ref[CONV_WIDTH - 1 - back:CONV_WIDTH - back, :]
                   * xbuf[pl.ds(V7X_SUBLANES - back, tm), :])
    xbuf[0:V7X_SUBLANES, :] = x_lru[tm - V7X_SUBLANES:, :]

    a, u = _lru_coeffs(xc, wg_ref, ba_ref, bx_ref, lam_ref)
    abuf[...] = a
    ubuf[...] = u

    row = lax.broadcasted_iota(jnp.int32, (V7X_SUBLANES, w), 0)

    def group(gi, h):
        r0 = pl.multiple_of(gi * V7X_SUBLANES, V7X_SUBLANES)
        ag = abuf[pl.ds(r0, V7X_SUBLANES), :]
        ug = ubuf[pl.ds(r0, V7X_SUBLANES), :]
        for d in (1, 2, 4):
            keep = row >= d
            ug = jnp.where(keep, ag * pltpu.roll(ug, d, 0) + ug, ug)
            ag = jnp.where(keep, ag * pltpu.roll(ag, d, 0), ag)
        hg = ag * h + ug
        ubuf[pl.ds(r0, V7X_SUBLANES), :] = hg
        return jnp.broadcast_to(hg[V7X_SUBLANES - 1:, :], (V7X_SUBLANES, w))

    hcar[...] = lax.fori_loop(0, tm // V7X_SUBLANES, group, hcar[...], unroll=2)
    hs = ubuf[...]
    ml_ref[...] = (_silu(g_lru) * hs).astype(BF16)

    @pl.when(t == pl.num_programs(1) - 1)
    def _():
        hl_ref[...] = hs[tm - 1:, :]
        cl_ref[...] = x_lru[tm - (CONV_WIDTH - 1):, :]


def _prompt_in(x, g_pre, whi, wlo, cw, cb, wg, ba, bx, lam, tm):
    b, s, d = x.shape
    w = LRU_WIDTH
    nb = s // MOBA_BLOCK
    nq = s // QUERY_BLOCK
    const = lambda shape: pl.BlockSpec(shape, lambda bi, ti: (0,) * len(shape))
    out_shape = (
        jax.ShapeDtypeStruct((b, nq, ATT_WIDTH, QUERY_BLOCK), F32),
        jax.ShapeDtypeStruct((b, ATT_WIDTH, s), F32),
        jax.ShapeDtypeStruct((b, ATT_WIDTH, s), F32),
        jax.ShapeDtypeStruct((b, nb, ATT_WIDTH, MOBA_BLOCK), BF16),
        jax.ShapeDtypeStruct((b, s, ATT_WIDTH), BF16),
        jax.ShapeDtypeStruct((b, nb, d), F32),
        jax.ShapeDtypeStruct((b, s, ATT_WIDTH), BF16),
        jax.ShapeDtypeStruct((b, s, w), BF16),
        jax.ShapeDtypeStruct((b, 1, w), F32),
        jax.ShapeDtypeStruct((b, CONV_WIDTH - 1, w), F32),
    )
    out_specs = (
        pl.BlockSpec((None, tm // QUERY_BLOCK, ATT_WIDTH, QUERY_BLOCK), lambda bi, ti: (bi, ti, 0, 0)),
        pl.BlockSpec((None, ATT_WIDTH, tm), lambda bi, ti: (bi, 0, ti)),
        pl.BlockSpec((None, ATT_WIDTH, tm), lambda bi, ti: (bi, 0, ti)),
        pl.BlockSpec((None, tm // MOBA_BLOCK, ATT_WIDTH, MOBA_BLOCK), lambda bi, ti: (bi, ti, 0, 0)),
        pl.BlockSpec((None, tm, ATT_WIDTH), lambda bi, ti: (bi, ti, 0)),
        pl.BlockSpec((None, nb, d), lambda bi, ti: (bi, 0, 0)),
        pl.BlockSpec((None, tm, ATT_WIDTH), lambda bi, ti: (bi, ti, 0)),
        pl.BlockSpec((None, tm, w), lambda bi, ti: (bi, ti, 0)),
        pl.BlockSpec((None, 1, w), lambda bi, ti: (bi, 0, 0)),
        pl.BlockSpec((None, CONV_WIDTH - 1, w), lambda bi, ti: (bi, 0, 0)),
    )
    in_specs = [
        pl.BlockSpec((None, tm, d), lambda bi, ti: (bi, ti, 0)),
        const((1, d)), const(whi.shape), const(wlo.shape), const(cw.shape), const(cb.shape),
        const(wg.shape), const(ba.shape), const(bx.shape), const(lam.shape),
    ]
    return pl.pallas_call(
        _prompt_in_kernel,
        grid=(b, s // tm),
        in_specs=in_specs,
        out_specs=out_specs,
        out_shape=out_shape,
        scratch_shapes=[
            pltpu.VMEM((tm + V7X_SUBLANES, w), F32),
            pltpu.VMEM((V7X_SUBLANES, w), F32),
            pltpu.VMEM((tm, w), F32),
            pltpu.VMEM((tm, w), F32),
        ],
        compiler_params=pltpu.CompilerParams(
            dimension_semantics=("arbitrary", "arbitrary"),
            vmem_limit_bytes=VMEM_LIMIT_BYTES),
        name="prompt_in",
    )(x, g_pre, whi, wlo, cw, cb, wg, ba, bx, lam)


def _prompt_attn_kernel(qt_ref, kn_ref, vtb_ref, xm_ref, wk_ref, o_ref, w_ref, sel_ref, sa_ref,
                        sb_ref, acc_ref):
    nsb = vtb_ref.shape[0]
    nb = xm_ref.shape[0]
    tq = MOBA_BLOCK
    hd = HEAD_DIM
    ones_rows = 2 * V7X_SUBLANES
    first = lax.broadcasted_iota(jnp.int32, (2 * hd, tq), 0) < hd
    kmp = _dot_exact(xm_ref[...], wk_ref[...])
    blk = lax.broadcasted_iota(jnp.int32, (nb, tq), 0)
    ones_tile = jnp.ones((ones_rows, tq), BF16)
    causal = (lax.broadcasted_iota(jnp.int32, (tq, tq), 0)
              <= lax.broadcasted_iota(jnp.int32, (tq, tq), 1))

    def stage(sb, c):
        qt = jnp.concatenate([qt_ref[2 * sb], qt_ref[2 * sb + 1]], axis=1)
        for h in range(2):
            qth = jnp.where(first, qt, 0.0) if h == 0 else jnp.where(first, 0.0, qt)
            gate = _dot_exact(kmp, qth)
            gate = jnp.where(blk < sb, gate, NEG_INF)
            sel_ref[sb, h] = _top3_rows(gate, blk, sb)
            w_ref[sb, :, h * tq:(h + 1) * tq] = (qth * (ATT_SCALE * LOG2_E)).astype(BF16)
        return c

    lax.fori_loop(0, nsb, stage, 0, unroll=2)

    def v_lhs(j, h):
        return jnp.concatenate([vtb_ref[j, h * hd:(h + 1) * hd, :], ones_tile], axis=0)

    def key_block(j):
        return kn_ref[pl.ds(pl.multiple_of(j * MOBA_BLOCK, MOBA_BLOCK), MOBA_BLOCK), :]

    heads = (0, 1)
    bufs = (sa_ref, sb_ref)
    hcols = tuple(slice(h * tq, (h + 1) * tq) for h in heads)

    def scores_into(buf, j, wq):
        sc = _dot(key_block(j), wq)
        buf[...] = sc
        return tuple(jnp.max(sc[:, hcols[h]], axis=0, keepdims=True) for h in heads)

    def super_block(sb, cm_first):
        last = sb - 1
        wq = w_ref[sb]
        for h in heads:
            acc_ref[h] = jnp.zeros(acc_ref.shape[1:], F32)
        m_init = tuple(jnp.full((1, tq), NEG_INF, F32) for _ in heads)

        def past_block(j, live, cur, nxt, carry):
            m_old, cm = carry
            cm_next = scores_into(bufs[nxt], jnp.minimum(j + 1, sb), wq)
            m_out = []
            for h in heads:
                on = sel_ref[sb, h, pl.ds(j, 1), :] * live > 0.0
                m_new = jnp.maximum(m_old[h], jnp.where(on, cm[h], NEG_INF))
                p = jnp.exp2(bufs[cur][:, hcols[h]] - jnp.where(on, m_new, POS_INF)).astype(BF16)
                alpha = jnp.exp2(m_old[h] - jnp.where(m_new == NEG_INF, 0.0, m_new))
                acc_ref[h] = acc_ref[h] * alpha + _dot(v_lhs(j, h), p)
                m_out.append(m_new)
            return tuple(m_out), cm_next

        def past_pair(j0, carry):
            j1 = jnp.minimum(j0 + 1, last)
            mid = past_block(j0, 1.0, 1, 0, carry)
            return past_block(j1, jnp.where(j0 + 1 < sb, 1.0, 0.0), 0, 1, mid)

        def past_run(j0, n_pairs, cr):
            for k in range(n_pairs):
                cr = past_pair(j0 + 2 * k, cr)
            return cr

        carry = (m_init, cm_first)
        done = 0
        for blocks in ATTN_TRIP_BLOCKS:
            n_trips = (sb - done) // blocks
            carry = lax.fori_loop(
                0, n_trips,
                lambda i, cr, done=done, blocks=blocks: past_run(done + blocks * i, blocks // 2, cr),
                carry)
            done = done + blocks * n_trips
        m_past, _ = lax.fori_loop(0, (sb - done + 1) // 2,
                                  lambda i, cr: past_pair(done + 2 * i, cr), carry)

        accs = []
        for h in heads:
            s = jnp.where(causal, bufs[1][:, hcols[h]], NEG_INF)
            m_new = jnp.maximum(m_past[h], jnp.max(s, axis=0, keepdims=True))
            p = jnp.exp2(s - m_new).astype(BF16)
            accs.append(acc_ref[h] * jnp.exp2(m_past[h] - m_new) + _dot(v_lhs(sb, h), p))
        cm_next = scores_into(bufs[1], 0, w_ref[jnp.minimum(sb + 1, nsb - 1)])
        ot = jnp.concatenate([a[:hd, :] / a[hd:hd + 1, :] for a in accs], axis=0)
        o_ref[pl.ds(pl.multiple_of(sb * tq, tq), tq), :] = ot.T
        return cm_next

    lax.fori_loop(0, nsb, super_block, scores_into(bufs[1], 0, w_ref[0]))


def _prompt_attn(qt, kn, vtb, xm, w_in):
    b, nq, _, qb = qt.shape
    s = kn.shape[1]
    nb, d = xm.shape[1:]
    pw = 2 * HEAD_DIM
    tq = MOBA_BLOCK
    k_col0 = ATT_WIDTH // pw
    return pl.pallas_call(
        _prompt_attn_kernel,
        grid=(b, HEAD_PAIRS),
        in_specs=[
            pl.BlockSpec((None, nq, pw, qb), lambda bi, pi: (bi, 0, pi, 0)),
            pl.BlockSpec((None, s, pw), lambda bi, pi: (bi, 0, pi)),
            pl.BlockSpec((None, nb, pw, tq), lambda bi, pi: (bi, 0, pi, 0)),
            pl.BlockSpec((None, nb, d), lambda bi, pi: (bi, 0, 0)),
            pl.BlockSpec((d, pw), lambda bi, pi: (0, k_col0 + pi)),
        ],
        out_specs=pl.BlockSpec((None, s, pw), lambda bi, pi: (bi, 0, pi)),
        out_shape=jax.ShapeDtypeStruct((b, s, ATT_WIDTH), F32),
        scratch_shapes=[
            pltpu.VMEM((nb, pw, 2 * tq), BF16),
            pltpu.VMEM((nb, 2, nb, tq), F32),
            pltpu.VMEM((tq, 2 * tq), F32), pltpu.VMEM((tq, 2 * tq), F32),
            pltpu.VMEM((2, HEAD_DIM + 2 * V7X_SUBLANES, tq), F32),
        ],
        compiler_params=pltpu.CompilerParams(
            dimension_semantics=("arbitrary", "arbitrary"),
            vmem_limit_bytes=VMEM_LIMIT_BYTES),
        name="prompt_attn",
    )(qt, kn, vtb, xm, w_in)


def _mix_out_ab(att, sg, ml, wo_ref):
    ma = (sg.astype(F32) * att).astype(BF16)
    return _dot(ma, wo_ref[:ATT_WIDTH, :]) + _dot(ml.astype(BF16), wo_ref[ATT_WIDTH:, :])


def _gmlp_in(y0, npre_ref, wi_ref, lg_ref, lb_ref):
    xn = _rms_norm(y0, npre_ref[...])
    pr = _dot(xn.astype(BF16), wi_ref[...])
    u = _gelu_tanh(pr[:, :GMLP_WIDTH])
    v = _layer_norm(_gelu_tanh(pr[:, GMLP_WIDTH:2 * GMLP_WIDTH]), lg_ref[...], lb_ref[...])
    g = pr[:, 2 * GMLP_WIDTH:]
    return u, v, g


def _prompt_out_kernel(x_ref, att_ref, sg_ref, ml_ref, np0_ref, wo_ref, npre_ref, wi_ref,
                       lg_ref, lb_ref, ws_ref, bst_ref, wc_ref, np1_ref, y_ref, mix_ref):
    tm = x_ref.shape[0]
    op = _mix_out_ab(att_ref[...], sg_ref[...], ml_ref[...], wo_ref)
    y0 = x_ref[...] + _rms_norm(op, np0_ref[...])
    u, v, g = _gmlp_in(y0, npre_ref, wi_ref, lg_ref, lb_ref)
    vb = v.astype(BF16)
    t_out = lax.broadcasted_iota(jnp.int32, (CHUNK, CHUNK), 0)
    t_in = lax.broadcasted_iota(jnp.int32, (CHUNK, CHUNK), 1)
    for gi in range(GMLP_GROUPS):
        wm = jnp.where(t_in <= t_out, ws_ref[gi], 0.0).astype(BF16)
        cols = slice(gi * GMLP_GROUP_DIM, (gi + 1) * GMLP_GROUP_DIM)
        for c in range(tm // CHUNK):
            rows = slice(c * CHUNK, (c + 1) * CHUNK)
            mix_ref[rows, cols] = _dot(wm, vb[rows, cols]) + bst_ref[:, cols]
    z = _silu(g) * (u * mix_ref[...])
    op1 = _dot(z.astype(BF16), wc_ref[...])
    y_ref[...] = y0 + _rms_norm(op1, np1_ref[...])


def _prompt_out(x2, att2, sg2, ml2, np0, wo, npre1, wi, lg, lb, ws, bst, wc, np1, tm):
    n, d = x2.shape
    const = lambda shape: pl.BlockSpec(shape, lambda i: (0,) * len(shape))
    rows = lambda width: pl.BlockSpec((tm, width), lambda i: (i, 0))
    return pl.pallas_call(
        _prompt_out_kernel,
        grid=(n // tm,),
        in_specs=[rows(d), rows(ATT_WIDTH), rows(ATT_WIDTH), rows(LRU_WIDTH),
                  const(np0.shape), const(wo.shape), const(npre1.shape), const(wi.shape),
                  const(lg.shape), const(lb.shape), const(ws.shape), const(bst.shape),
                  const(wc.shape), const(np1.shape)],
        out_specs=rows(d),
        out_shape=jax.ShapeDtypeStruct((n, d), F32),
        scratch_shapes=[pltpu.VMEM((tm, GMLP_WIDTH), F32)],
        compiler_params=pltpu.CompilerParams(
            dimension_semantics=("arbitrary",),
            vmem_limit_bytes=VMEM_LIMIT_BYTES),
        name="prompt_out",
    )(x2, att2, sg2, ml2, np0, wo, npre1, wi, lg, lb, ws, bst, wc, np1)


def _sample_in_kernel(x_ref, g_ref, whi_ref, wlo_ref, cw_ref, cb_ref, wg_ref,
                      ba_ref, bx_ref, lam_ref, h0_ref, st_ref,
                      q_ref, k_ref, v_ref, sg_ref, ml_ref, hs_ref, xl_ref):
    n = x_ref.shape[0]
    t_len = V7X_SUBLANES
    w = LRU_WIDTH
    xn = _rms_norm(x_ref[...], g_ref[...])
    q, k, rest = _in_proj_ab(xn, whi_ref, wlo_ref)
    q_ref[...] = q
    k_ref[...] = k
    v_ref[...] = rest[:, :ATT_WIDTH]
    g_att = rest[:, ATT_WIDTH:2 * ATT_WIDTH]
    x_lru = rest[:, 2 * ATT_WIDTH:2 * ATT_WIDTH + w]
    g_lru = rest[:, 2 * ATT_WIDTH + w:]
    sg_ref[...] = _silu(g_att)
    xl_ref[...] = x_lru

    tok = lax.broadcasted_iota(jnp.int32, (n, w), 0) % t_len
    st = st_ref[...]
    xc = cw_ref[CONV_WIDTH - 1:CONV_WIDTH, :] * x_lru + cb_ref[...]
    for back in range(1, CONV_WIDTH):
        prev = jnp.where(tok >= back, pltpu.roll(x_lru, back, 0),
                         pltpu.roll(st, n - t_len + back, 0))
        xc = xc + cw_ref[CONV_WIDTH - 1 - back:CONV_WIDTH - back, :] * prev

    a, u = _lru_coeffs(xc, wg_ref, ba_ref, bx_ref, lam_ref)
    for d in (1, 2, 4):
        keep = tok >= d
        u = jnp.where(keep, a * pltpu.roll(u, d, 0) + u, u)
        a = jnp.where(keep, a * pltpu.roll(a, d, 0), a)
    hs = a * h0_ref[...] + u
    hs_ref[...] = hs
    ml_ref[...] = _silu(g_lru) * hs


def _sample_in(xs, g_pre, whi, wlo, cw, cb, wg, ba, bx, lam, h0rep, stpad):
    n, d = xs.shape
    w = LRU_WIDTH
    args = (xs, g_pre, whi, wlo, cw, cb, wg, ba, bx, lam, h0rep, stpad)
    full = lambda a: pl.BlockSpec(a.shape, lambda i: (0,) * a.ndim)
    outs = [jax.ShapeDtypeStruct((n, ATT_WIDTH), F32)] * 4 + [jax.ShapeDtypeStruct((n, w), F32)] * 3
    return pl.pallas_call(
        _sample_in_kernel,
        grid=(1,),
        in_specs=[full(a) for a in args],
        out_specs=tuple(pl.BlockSpec(o.shape, lambda i: (0, 0)) for o in outs),
        out_shape=tuple(outs),
        compiler_params=pltpu.CompilerParams(
            dimension_semantics=("arbitrary",), vmem_limit_bytes=VMEM_LIMIT_BYTES),
        name="sample_in",
    )(*args)


def _sample_select_kernel(pt_ref, q_ref, ptv_ref, *refs, pages_per_step, n_blocks):
    k_refs = refs[:pages_per_step]
    idx_ref = refs[pages_per_step]
    kmt = refs[pages_per_step + 1]
    c = pl.program_id(1)
    bp = MOBA_BLOCK // PAGE_SIZE
    lane3 = lax.broadcasted_iota(jnp.int32, kmt.shape, 2)

    @pl.when(c == 0)
    def _():
        kmt[...] = jnp.zeros(kmt.shape, F32)

    for i in range(pages_per_step // bp):
        tot = k_refs[bp * i][...]
        for pg in range(1, bp):
            tot = tot + k_refs[bp * i + pg][...]
        col = jnp.sum(tot, axis=-1, keepdims=True) * (1.0 / MOBA_BLOCK)
        blk = c * (pages_per_step // bp) + i
        kmt[...] = jnp.where(lane3 == blk, col, kmt[...])

    @pl.when(c == pl.num_programs(1) - 1)
    def _():
        qv = q_ref[...]
        t_len = qv.shape[0]
        lane = lax.broadcasted_iota(jnp.int32, (t_len, V7X_LANES), 1)
        lane_pair = lax.broadcasted_iota(jnp.int32, (t_len, 2 * HEAD_DIM), 1)
        pages = jnp.broadcast_to(ptv_ref[...].astype(F32), (t_len, V7X_LANES))
        for p in range(HEAD_PAIRS):
            qp = qv[:, p * 2 * HEAD_DIM:(p + 1) * 2 * HEAD_DIM]
            kmp = kmt[2 * p:2 * p + 2].reshape(2 * HEAD_DIM, V7X_LANES)
            for hh in range(2):
                qm = jnp.where((lane_pair < HEAD_DIM) == (hh == 0), qp, 0.0)
                gate = _dot_exact(qm, kmp)
                gate = jnp.where(lane < n_blocks, gate, NEG_INF)
                out = jnp.zeros((t_len, V7X_LANES), jnp.int32)
                for r in range(MOBA_TOPK):
                    mx = jnp.max(gate, axis=-1, keepdims=True)
                    ix = jnp.min(jnp.where(gate == mx, lane, V7X_LANES), axis=-1, keepdims=True)
                    gate = jnp.where(lane == ix, NEG_INF, gate)
                    for pg in range(bp):
                        phys = jnp.sum(jnp.where(lane == ix * bp + pg, pages, 0.0),
                                       axis=-1, keepdims=True)
                        out = jnp.where(lane == r * bp + pg, phys.astype(jnp.int32), out)
                idx_ref[2 * p + hh] = out


def _sample_select(page_table, q_s, cache_t, layer, pages_per_step):
    db, n_pages = page_table.shape
    t_len = q_s.shape[0] // db
    n_blocks = n_pages * PAGE_SIZE // MOBA_BLOCK
    assert n_pages == V7X_LANES and n_pages % pages_per_step == 0

    def page_spec(i):
        return pl.BlockSpec(
            (None, None, ATT_HEADS, HEAD_DIM, PAGE_SIZE),
            lambda b, c, pt: (layer, pt[b * n_pages + c * pages_per_step + i], 0, 0, 0))

    grid_spec = pltpu.PrefetchScalarGridSpec(
        num_scalar_prefetch=1,
        grid=(db, n_pages // pages_per_step),
        in_specs=[pl.BlockSpec((t_len, ATT_WIDTH), lambda b, c, pt: (b, 0)),
                  pl.BlockSpec((None, 1, n_pages), lambda b, c, pt: (b, 0, 0))]
                 + [page_spec(i) for i in range(pages_per_step)],
        out_specs=pl.BlockSpec((None, ATT_HEADS, t_len, V7X_LANES), lambda b, c, pt: (b, 0, 0, 0)),
        scratch_shapes=[pltpu.VMEM((ATT_HEADS, HEAD_DIM, V7X_LANES), F32)],
    )
    return pl.pallas_call(
        functools.partial(_sample_select_kernel, pages_per_step=pages_per_step, n_blocks=n_blocks),
        grid_spec=grid_spec,
        out_shape=jax.ShapeDtypeStruct((db, ATT_HEADS, t_len, V7X_LANES), jnp.int32),
        compiler_params=pltpu.CompilerParams(
            dimension_semantics=("arbitrary", "arbitrary"), vmem_limit_bytes=VMEM_LIMIT_BYTES),
        name="sample_select",
    )(page_table.reshape(-1), q_s, page_table.reshape(db, 1, n_pages),
      *([cache_t] * pages_per_step))


def _sample_attn_kernel(ph_ref, qt_ref, kt_ref, vt_ref, kc_ref, vc_ref, o_ref, kbuf, vbuf, sem,
                        *, t_len, layer):
    n_sel = MOBA_TOPK * (MOBA_BLOCK // PAGE_SIZE)
    n_tiles = t_len * n_sel
    heads = pl.num_programs(1)
    step = pl.program_id(0) * heads + pl.program_id(1)
    n_steps = pl.num_programs(0) * heads
    slot = step % 2

    def tile_copies(step_i, slot_i, i):
        page = ph_ref[step_i * n_tiles + i]
        head = step_i % heads
        return (pltpu.make_async_copy(kc_ref.at[layer, page, head], kbuf.at[slot_i, i], sem.at[0, slot_i]),
                pltpu.make_async_copy(vc_ref.at[layer, page, head], vbuf.at[slot_i, i], sem.at[1, slot_i]))

    def start_all(step_i, slot_i):
        def body(i, c):
            for cp in tile_copies(step_i, slot_i, i):
                cp.start()
            return c
        lax.fori_loop(0, n_tiles, body, 0, unroll=4)

    @pl.when(step == 0)
    def _():
        start_all(0, 0)

    @pl.when(step + 1 < n_steps)
    def _():
        start_all(step + 1, 1 - slot)

    def wait_tile(i, c):
        for cp in tile_copies(step, slot, i):
            cp.wait()
        return c

    lax.fori_loop(0, n_tiles, wait_tile, 0, unroll=4)

    ktn = kt_ref[...]
    vtn = vt_ref[...]
    qtn = qt_ref[...] * ATT_SCALE
    row_sel = lax.broadcasted_iota(jnp.int32, (t_len, PAGE_SIZE), 0)
    row_own = lax.broadcasted_iota(jnp.int32, (t_len, t_len), 0)
    col_own = lax.broadcasted_iota(jnp.int32, (t_len, t_len), 1)
    s_sel = [jnp.zeros((t_len, PAGE_SIZE), F32) for _ in range(n_sel)]
    s_own = jnp.zeros((t_len, t_len), F32)
    for t in range(t_len):
        qc = qtn[:, t:t + 1]
        for i in range(n_sel):
            s = jnp.sum(kbuf[slot, t * n_sel + i] * qc, axis=0, keepdims=True)
            s_sel[i] = jnp.where(row_sel == t, s, s_sel[i])
        s_own = jnp.where(row_own == t, jnp.sum(ktn * qc, axis=0, keepdims=True), s_own)
    s_own = jnp.where(col_own <= row_own, s_own, NEG_INF)
    m_sel = s_sel[0]
    for i in range(1, n_sel):
        m_sel = jnp.maximum(m_sel, s_sel[i])
    m = jnp.maximum(jnp.max(m_sel, axis=-1, keepdims=True), jnp.max(s_own, axis=-1, keepdims=True))
    p_sel = [jnp.exp(s - m) for s in s_sel]
    p_own = jnp.exp(s_own - m)
    p_tot = p_sel[0]
    for i in range(1, n_sel):
        p_tot = p_tot + p_sel[i]
    den = jnp.sum(p_tot, axis=-1, keepdims=True) + jnp.sum(p_own, axis=-1, keepdims=True)
    for t in range(t_len):
        acc = vbuf[slot, t * n_sel] * p_sel[0][t:t + 1, :]
        for i in range(1, n_sel):
            acc = acc + vbuf[slot, t * n_sel + i] * p_sel[i][t:t + 1, :]
        o = (jnp.sum(acc, axis=-1, keepdims=True)
             + jnp.sum(vtn * p_own[t:t + 1, :], axis=-1, keepdims=True))
        o_ref[:, t:t + 1] = o / den[t:t + 1, :]


def _sample_attn(phys, qt_s, kt_s, vt_s, cache_kt, cache_vt, layer):
    db, _, t_len = qt_s.shape
    n_tiles = t_len * MOBA_TOPK * (MOBA_BLOCK // PAGE_SIZE)
    new_spec = pl.BlockSpec((None, HEAD_DIM, t_len), lambda b, h, ph: (b, h, 0))
    cache_spec = pl.BlockSpec(memory_space=pl.ANY)
    grid_spec = pltpu.PrefetchScalarGridSpec(
        num_scalar_prefetch=1,
        grid=(db, ATT_HEADS),
        in_specs=[new_spec, new_spec, new_spec, cache_spec, cache_spec],
        out_specs=new_spec,
        scratch_shapes=[
            pltpu.VMEM((2, n_tiles, HEAD_DIM, PAGE_SIZE), F32),
            pltpu.VMEM((2, n_tiles, HEAD_DIM, PAGE_SIZE), F32),
            pltpu.SemaphoreType.DMA((2, 2)),
        ],
    )
    return pl.pallas_call(
        functools.partial(_sample_attn_kernel, t_len=t_len, layer=layer),
        grid_spec=grid_spec,
        out_shape=jax.ShapeDtypeStruct((db, ATT_WIDTH, t_len), F32),
        compiler_params=pltpu.CompilerParams(
            dimension_semantics=("arbitrary", "arbitrary"), vmem_limit_bytes=VMEM_LIMIT_BYTES),
        name="sample_attn",
    )(phys.reshape(-1), qt_s, kt_s, vt_s, cache_kt, cache_vt)


def _sample_out_kernel(x_ref, att_ref, sg_ref, ml_ref, np0_ref, wo_ref, npre_ref, wi_ref,
                       lg_ref, lb_ref, cd_ref, bst_ref, wc_ref, np1_ref, y_ref, gv_ref):
    n = x_ref.shape[0]
    t_len = cd_ref.shape[1]
    reps = n // t_len
    op = _mix_out_ab(att_ref[...], sg_ref[...], ml_ref[...], wo_ref)
    y0 = x_ref[...] + _rms_norm(op, np0_ref[...])
    u, v, g = _gmlp_in(y0, npre_ref, wi_ref, lg_ref, lb_ref)
    gv_ref[...] = v
    tile = lambda tab: jnp.concatenate([tab] * reps, axis=0)
    mix = tile(bst_ref[...]) + tile(cd_ref[0]) * v
    for d in range(1, t_len):
        mix = mix + tile(cd_ref[d]) * pltpu.roll(v, d, 0)
    z = _silu(g) * (u * mix)
    op1 = _dot(z.astype(BF16), wc_ref[...])
    y_ref[...] = y0 + _rms_norm(op1, np1_ref[...])


def _sample_out(xs, att, sg, ml, np0, wo, npre1, wi, lg, lb, cd, bst8, wc, np1):
    n, d = xs.shape
    args = (xs, att, sg, ml, np0, wo, npre1, wi, lg, lb, cd, bst8, wc, np1)
    full = lambda a: pl.BlockSpec(a.shape, lambda i: (0,) * a.ndim)
    outs = (jax.ShapeDtypeStruct((n, d), F32), jax.ShapeDtypeStruct((n, GMLP_WIDTH), F32))
    return pl.pallas_call(
        _sample_out_kernel,
        grid=(1,),
        in_specs=[full(a) for a in args],
        out_specs=tuple(pl.BlockSpec(o.shape, lambda i: (0, 0)) for o in outs),
        out_shape=outs,
        compiler_params=pltpu.CompilerParams(
            dimension_semantics=("arbitrary",), vmem_limit_bytes=VMEM_LIMIT_BYTES),
        name="sample_out",
    )(*args)


def _gate_weights(wa, wx):
    h, n, _ = wa.shape
    per = V7X_LANES // n
    eye = jnp.eye(per, dtype=wa.dtype)

    def grouped(wh):
        blocks = wh.reshape(h // per, per, n, n)
        return jnp.einsum("gpij,pq->gpiqj", blocks, eye).reshape(h // per, per * n, per * n)

    return jnp.concatenate([grouped(wa), grouped(wx)], axis=2)


def kernel(x_prompt, x_sample, cache_k, cache_v, page_table, state_lru_h, state_conv, norm_pre, norm_post, w_in_ab, conv_w, conv_b, lru_wa, lru_ba, lru_wx, lru_bx, lru_lambda, w_out_ab, w_in_c, c_ln_g, c_ln_b, c_ws, c_bs, w_out_c):
    b, s, d = x_prompt.shape
    db, t_len, _ = x_sample.shape
    n_pages = page_table.shape[1]
    assert norm_pre.shape[0] == 2 and w_in_ab.shape[0] == 1 and w_in_c.shape[0] == 1
    assert s % (2 * MOBA_BLOCK) == 0 and t_len == V7X_SUBLANES
    assert (n_pages * PAGE_SIZE) % MOBA_BLOCK == 0 and t_len <= CHUNK
    assert cache_k.shape[2:] == (PAGE_SIZE, ATT_HEADS, HEAD_DIM)
    w = LRU_WIDTH
    row = lambda vec: vec.reshape(1, -1)

    whi = w_in_ab[0].astype(BF16)
    wq = w_in_ab[0][:, :ATT_WIDTH]
    wlo = (wq - wq.astype(BF16).astype(F32)).astype(BF16)
    wg = _gate_weights(lru_wa[0], lru_wx[0]).astype(BF16)
    lru_args = (conv_w[0], row(conv_b[0]), wg, row(lru_ba[0]), row(lru_bx[0]), row(lru_lambda[0]))
    wo = w_out_ab[0].astype(BF16)
    wi = w_in_c[0].astype(BF16)
    wc = w_out_c[0].astype(BF16)
    np0, np1 = row(norm_post[0]), row(norm_post[1])
    npre0, npre1 = row(norm_pre[0]), row(norm_pre[1])
    lg, lb = row(c_ln_g[0]), row(c_ln_b[0])
    bst = jnp.repeat(c_bs[0].T, GMLP_GROUP_DIM, axis=1)

    (qt, kt, vt, vtb, kn, xm, sg, ml, h_last, conv_last) = _prompt_in(
        x_prompt, npre0, whi, wlo, *lru_args, tm=2 * MOBA_BLOCK)
    att = _prompt_attn(qt, kn, vtb, xm, w_in_ab[0])
    y_prompt = _prompt_out(
        x_prompt.reshape(b * s, d), att.reshape(b * s, ATT_WIDTH), sg.reshape(b * s, ATT_WIDTH),
        ml.reshape(b * s, w), np0, wo, npre1, wi, lg, lb, c_ws[0], bst, wc, np1,
        tm=2 * MOBA_BLOCK).reshape(b, s, d)
    heads_last = lambda xt: xt.reshape(b, ATT_HEADS, HEAD_DIM, s).transpose(0, 3, 1, 2)[None]
    k_prompt, v_prompt = heads_last(kt), heads_last(vt)

    n = db * t_len
    xs = x_sample.reshape(n, d)
    h0rep = jnp.repeat(state_lru_h[0], t_len, axis=0)
    stpad = jnp.pad(state_conv[0], ((0, 0), (t_len - (CONV_WIDTH - 1), 0), (0, 0))).reshape(n, w)
    q_s, k_s, v_s, sg_s, ml_s, hs_s, xl_s = _sample_in(xs, npre0, whi, wlo, *lru_args, h0rep, stpad)

    cache_kt = cache_k.transpose(0, 1, 3, 4, 2)
    cache_vt = cache_v.transpose(0, 1, 3, 4, 2)
    phys = _sample_select(page_table, q_s, cache_kt, 0, pages_per_step=32)
    tok_last = lambda a: a.reshape(db, t_len, ATT_WIDTH).transpose(0, 2, 1)
    att_t = _sample_attn(phys[..., :MOBA_TOPK * (MOBA_BLOCK // PAGE_SIZE)], tok_last(q_s),
                         tok_last(k_s), tok_last(v_s), cache_kt, cache_vt, 0)
    att_s = att_t.transpose(0, 2, 1).reshape(n, ATT_WIDTH)

    ws8 = c_ws[0][:, :t_len, :t_len]
    tt = jnp.arange(t_len)
    diag = lambda dd: jnp.where(tt >= dd, ws8[:, tt, jnp.maximum(tt - dd, 0)], 0.0)
    cd = jnp.stack([jnp.repeat(diag(dd).T, GMLP_GROUP_DIM, axis=1) for dd in range(t_len)])
    y_s, gv_s = _sample_out(xs, att_s, sg_s, ml_s, np0, wo, npre1, wi, lg, lb, cd, bst[:t_len],
                            wc, np1)

    per_req = lambda a, width: a.reshape(db, t_len, width)
    return (y_prompt, y_s.reshape(db, t_len, d), k_prompt, v_prompt,
            h_last.reshape(1, b, w), conv_last[None],
            k_s.reshape(1, db, t_len, ATT_HEADS, HEAD_DIM), v_s.reshape(1, db, t_len, ATT_HEADS, HEAD_DIM),
            per_req(hs_s, w)[:, t_len - 1][None], per_req(xl_s, w)[:, t_len - (CONV_WIDTH - 1):][None],
            per_req(gv_s, GMLP_WIDTH)[None])
```

```python
import functools

import jax
import jax.numpy as jnp
from jax import lax
from jax.experimental import pallas as pl
from jax.experimental.pallas import tpu as pltpu

ATT_HEADS = 8
HEAD_DIM = 64
ATT_WIDTH = ATT_HEADS * HEAD_DIM
HEAD_PAIRS = ATT_HEADS // 2
MOBA_BLOCK = 256
MOBA_TOPK = 3
QUERY_BLOCK = 128
LRU_WIDTH = 512
CONV_WIDTH = 4
LRU_C = 8.0
GMLP_WIDTH = 1024
GMLP_GROUPS = 8
GMLP_GROUP_DIM = GMLP_WIDTH // GMLP_GROUPS
CHUNK = 128
PAGE_SIZE = 128
NORM_EPS = 1e-6
ATT_SCALE = HEAD_DIM ** -0.5
LOG2_E = 1.4426950408889634

ATTN_TRIP_BLOCKS = (8, 4)
V7X_LANES = 128
V7X_SUBLANES = 8
VMEM_LIMIT_BYTES = 56 * 1024 * 1024

F32 = jnp.float32
BF16 = jnp.bfloat16
NEG_INF = float("-inf")
POS_INF = float("inf")


def _rms_norm(x, g):
    return x * lax.rsqrt(jnp.mean(x * x, axis=-1, keepdims=True) + NORM_EPS) * g


def _layer_norm(x, g, b):
    mu = jnp.mean(x, axis=-1, keepdims=True)
    xc = x - mu
    var = jnp.mean(xc * xc, axis=-1, keepdims=True)
    return xc * lax.rsqrt(var + NORM_EPS) * g + b


def _sigmoid(x):
    return 1.0 / (1.0 + jnp.exp(-x))


def _silu(x):
    return x * _sigmoid(x)


def _gelu_tanh(x):
    c = 0.7978845608028654
    return 0.5 * x * (1.0 + jnp.tanh(c * (x + 0.044715 * (x * x * x))))


def _softplus(x):
    return jnp.maximum(x, 0.0) + jnp.log1p(jnp.exp(-jnp.abs(x)))


def _dot(a, b):
    return jnp.dot(a, b, preferred_element_type=F32)


def _dot_exact(a, b):
    return jnp.dot(a, b, preferred_element_type=F32, precision=lax.Precision.HIGHEST)


def _split_bf16(x):
    hi = x.astype(BF16)
    lo = (x - hi.astype(F32)).astype(BF16)
    return hi, lo


def _in_proj_ab(xn, whi_ref, wlo_ref):
    xh, xl = _split_bf16(xn)
    q = (_dot(xh, whi_ref[:, :ATT_WIDTH]) + _dot(xh, wlo_ref[...])
         + _dot(xl, whi_ref[:, :ATT_WIDTH]))
    rest = _dot(xh, whi_ref[:, ATT_WIDTH:])
    return q, rest[:, :ATT_WIDTH], rest[:, ATT_WIDTH:]


def _lru_coeffs(xc, wg_ref, ba_ref, bx_ref, lam_ref):
    xcb = xc.astype(BF16)
    lanes = V7X_LANES
    pre = [_dot(xcb[:, g * lanes:(g + 1) * lanes], wg_ref[g]) for g in range(wg_ref.shape[0])]
    r = _sigmoid(jnp.concatenate([p[:, :lanes] for p in pre], axis=1) + ba_ref[...])
    i = _sigmoid(jnp.concatenate([p[:, lanes:] for p in pre], axis=1) + bx_ref[...])
    log_a = (-LRU_C) * r * _softplus(-lam_ref[...])
    a = jnp.exp(log_a)
    u = jnp.sqrt(-jnp.tanh(log_a) * (a * a + 1.0)) * (i * xc)
    return a, u


def _top3_rows(gate, blk, n_valid):
    nb = gate.shape[0]
    sel = jnp.zeros(gate.shape, F32)
    for r in range(MOBA_TOPK):
        mx = jnp.max(gate, axis=0, keepdims=True)
        ix = jnp.min(jnp.where(gate == mx, blk, nb), axis=0, keepdims=True)
        hit = blk == ix
        sel = jnp.maximum(sel, jnp.where(hit, jnp.where(r < n_valid, 1.0, 0.0), 0.0))
        gate = jnp.where(hit, NEG_INF, gate)
    return sel


def _prompt_in_kernel(x_ref, g_ref, whi_ref, wlo_ref, cw_ref, cb_ref, wg_ref,
                      ba_ref, bx_ref, lam_ref,
                      qt_ref, kt_ref, vt_ref, vtb_ref, kn_ref, xm_ref, sg_ref, ml_ref,
                      hl_ref, cl_ref,
                      xbuf, hcar, abuf, ubuf):
    t = pl.program_id(1)
    tm = x_ref.shape[0]
    w = LRU_WIDTH

    @pl.when(t == 0)
    def _():
        xbuf[0:V7X_SUBLANES, :] = jnp.zeros((V7X_SUBLANES, w), F32)
        hcar[...] = jnp.zeros(hcar.shape, F32)

    xn = _rms_norm(x_ref[...], g_ref[...])
    q, k, rest = _in_proj_ab(xn, whi_ref, wlo_ref)
    v = rest[:, :ATT_WIDTH]
    g_att = rest[:, ATT_WIDTH:2 * ATT_WIDTH]
    x_lru = rest[:, 2 * ATT_WIDTH:2 * ATT_WIDTH + w]
    g_lru = rest[:, 2 * ATT_WIDTH + w:]

    for s in range(tm // QUERY_BLOCK):
        rows = slice(s * QUERY_BLOCK, (s + 1) * QUERY_BLOCK)
        qt_ref[s] = q[rows, :].T
    kt_ref[...] = k.T
    vt = v.T
    vt_ref[...] = vt
    kn_ref[...] = k.astype(BF16)
    for s in range(tm // MOBA_BLOCK):
        vtb_ref[s] = vt[:, s * MOBA_BLOCK:(s + 1) * MOBA_BLOCK].astype(BF16)
        xm_ref[pl.ds(t * (tm // MOBA_BLOCK) + s, 1), :] = jnp.mean(
            xn[s * MOBA_BLOCK:(s + 1) * MOBA_BLOCK, :], axis=0, keepdims=True)
    sg_ref[...] = _silu(g_att).astype(BF16)

    xbuf[V7X_SUBLANES:, :] = x_lru
    xc = cw_ref[CONV_WIDTH - 1:CONV_WIDTH, :] * x_lru + cb_ref[...]
    for back in range(1, CONV_WIDTH):
        xc = xc + (cw_ref[CONV_WIDTH - 1 - back:CONV_WIDTH - back, :]
                   * xbuf[pl.ds(V7X_SUBLANES - back, tm), :])
    xbuf[0:V7X_SUBLANES, :] = x_lru[tm - V7X_SUBLANES:, :]

    a, u = _lru_coeffs(xc, wg_ref, ba_ref, bx_ref, lam_ref)
    abuf[...] = a
    ubuf[...] = u

    row = lax.broadcasted_iota(jnp.int32, (V7X_SUBLANES, w), 0)

    def group(gi, h):
        r0 = pl.multiple_of(gi * V7X_SUBLANES, V7X_SUBLANES)
        ag = abuf[pl.ds(r0, V7X_SUBLANES), :]
        ug = ubuf[pl.ds(r0, V7X_SUBLANES), :]
        for d in (1, 2, 4):
            keep = row >= d
            ug = jnp.where(keep, ag * pltpu.roll(ug, d, 0) + ug, ug)
            ag = jnp.where(keep, ag * pltpu.roll(ag, d, 0), ag)
        hg = ag * h + ug
        ubuf[pl.ds(r0, V7X_SUBLANES), :] = hg
        return jnp.broadcast_to(hg[V7X_SUBLANES - 1:, :], (V7X_SUBLANES, w))

    hcar[...] = lax.fori_loop(0, tm // V7X_SUBLANES, group, hcar[...], unroll=2)
    hs = ubuf[...]
    ml_ref[...] = (_silu(g_lru) * hs).astype(BF16)

    @pl.when(t == pl.num_programs(1) - 1)
    def _():
        hl_ref[...] = hs[tm - 1:, :]
        cl_ref[...] = x_lru[tm - (CONV_WIDTH - 1):, :]


def _prompt_in(x, g_pre, whi, wlo, cw, cb, wg, ba, bx, lam, tm):
    b, s, d = x.shape
    w = LRU_WIDTH
    nb = s // MOBA_BLOCK
    nq = s // QUERY_BLOCK
    const = lambda shape: pl.BlockSpec(shape, lambda bi, ti: (0,) * len(shape))
    out_shape = (
        jax.ShapeDtypeStruct((b, nq, ATT_WIDTH, QUERY_BLOCK), F32),
        jax.ShapeDtypeStruct((b, ATT_WIDTH, s), F32),
        jax.ShapeDtypeStruct((b, ATT_WIDTH, s), F32),
        jax.ShapeDtypeStruct((b, nb, ATT_WIDTH, MOBA_BLOCK), BF16),
        jax.ShapeDtypeStruct((b, s, ATT_WIDTH), BF16),
        jax.ShapeDtypeStruct((b, nb, d), F32),
        jax.ShapeDtypeStruct((b, s, ATT_WIDTH), BF16),
        jax.ShapeDtypeStruct((b, s, w), BF16),
        jax.ShapeDtypeStruct((b, 1, w), F32),
        jax.ShapeDtypeStruct((b, CONV_WIDTH - 1, w), F32),
    )
    out_specs = (
        pl.BlockSpec((None, tm // QUERY_BLOCK, ATT_WIDTH, QUERY_BLOCK), lambda bi, ti: (bi, ti, 0, 0)),
        pl.BlockSpec((None, ATT_WIDTH, tm), lambda bi, ti: (bi, 0, ti)),
        pl.BlockSpec((None, ATT_WIDTH, tm), lambda bi, ti: (bi, 0, ti)),
        pl.BlockSpec((None, tm // MOBA_BLOCK, ATT_WIDTH, MOBA_BLOCK), lambda bi, ti: (bi, ti, 0, 0)),
        pl.BlockSpec((None, tm, ATT_WIDTH), lambda bi, ti: (bi, ti, 0)),
        pl.BlockSpec((None, nb, d), lambda bi, ti: (bi, 0, 0)),
        pl.BlockSpec((None, tm, ATT_WIDTH), lambda bi, ti: (bi, ti, 0)),
        pl.BlockSpec((None, tm, w), lambda bi, ti: (bi, ti, 0)),
        pl.BlockSpec((None, 1, w), lambda bi, ti: (bi, 0, 0)),
        pl.BlockSpec((None, CONV_WIDTH - 1, w), lambda bi, ti: (bi, 0, 0)),
    )
    in_specs = [
        pl.BlockSpec((None, tm, d), lambda bi, ti: (bi, ti, 0)),
        const((1, d)), const(whi.shape), const(wlo.shape), const(cw.shape), const(cb.shape),
        const(wg.shape), const(ba.shape), const(bx.shape), const(lam.shape),
    ]
    return pl.pallas_call(
        _prompt_in_kernel,
        grid=(b, s // tm),
        in_specs=in_specs,
        out_specs=out_specs,
        out_shape=out_shape,
        scratch_shapes=[
            pltpu.VMEM((tm + V7X_SUBLANES, w), F32),
            pltpu.VMEM((V7X_SUBLANES, w), F32),
            pltpu.VMEM((tm, w), F32),
            pltpu.VMEM((tm, w), F32),
        ],
        compiler_params=pltpu.CompilerParams(
            dimension_semantics=("arbitrary", "arbitrary"),
            vmem_limit_bytes=VMEM_LIMIT_BYTES),
        name="prompt_in",
    )(x, g_pre, whi, wlo, cw, cb, wg, ba, bx, lam)


def _prompt_attn_kernel(qt_ref, kn_ref, vtb_ref, xm_ref, wk_ref, o_ref, w_ref, sel_ref, sa_ref,
                        sb_ref, acc_ref):
    nsb = vtb_ref.shape[0]
    nb = xm_ref.shape[0]
    tq = MOBA_BLOCK
    hd = HEAD_DIM
    ones_rows = 2 * V7X_SUBLANES
    first = lax.broadcasted_iota(jnp.int32, (2 * hd, tq), 0) < hd
    kmp = _dot_exact(xm_ref[...], wk_ref[...])
    blk = lax.broadcasted_iota(jnp.int32, (nb, tq), 0)
    ones_tile = jnp.ones((ones_rows, tq), BF16)
    causal = (lax.broadcasted_iota(jnp.int32, (tq, tq), 0)
              <= lax.broadcasted_iota(jnp.int32, (tq, tq), 1))

    def stage(sb, c):
        qt = jnp.concatenate([qt_ref[2 * sb], qt_ref[2 * sb + 1]], axis=1)
        for h in range(2):
            qth = jnp.where(first, qt, 0.0) if h == 0 else jnp.where(first, 0.0, qt)
            gate = _dot_exact(kmp, qth)
            gate = jnp.where(blk < sb, gate, NEG_INF)
            sel_ref[sb, h] = _top3_rows(gate, blk, sb)
            w_ref[sb, :, h * tq:(h + 1) * tq] = (qth * (ATT_SCALE * LOG2_E)).astype(BF16)
        return c

    lax.fori_loop(0, nsb, stage, 0, unroll=2)

    def v_lhs(j, h):
        return jnp.concatenate([vtb_ref[j, h * hd:(h + 1) * hd, :], ones_tile], axis=0)

    def key_block(j):
        return kn_ref[pl.ds(pl.multiple_of(j * MOBA_BLOCK, MOBA_BLOCK), MOBA_BLOCK), :]

    heads = (0, 1)
    bufs = (sa_ref, sb_ref)
    hcols = tuple(slice(h * tq, (h + 1) * tq) for h in heads)

    def scores_into(buf, j, wq):
        sc = _dot(key_block(j), wq)
        buf[...] = sc
        return tuple(jnp.max(sc[:, hcols[h]], axis=0, keepdims=True) for h in heads)

    def super_block(sb, cm_first):
        last = sb - 1
        wq = w_ref[sb]
        for h in heads:
            acc_ref[h] = jnp.zeros(acc_ref.shape[1:], F32)
        m_init = tuple(jnp.full((1, tq), NEG_INF, F32) for _ in heads)

        def past_block(j, live, cur, nxt, carry):
            m_old, cm = carry
            cm_next = scores_into(bufs[nxt], jnp.minimum(j + 1, sb), wq)
            m_out = []
            for h in heads:
                on = sel_ref[sb, h, pl.ds(j, 1), :] * live > 0.0
                m_new = jnp.maximum(m_old[h], jnp.where(on, cm[h], NEG_INF))
                p = jnp.exp2(bufs[cur][:, hcols[h]] - jnp.where(on, m_new, POS_INF)).astype(BF16)
                alpha = jnp.exp2(m_old[h] - jnp.where(m_new == NEG_INF, 0.0, m_new))
                acc_ref[h] = acc_ref[h] * alpha + _dot(v_lhs(j, h), p)
                m_out.append(m_new)
            return tuple(m_out), cm_next

        def past_pair(j0, carry):
            j1 = jnp.minimum(j0 + 1, last)
            mid = past_block(j0, 1.0, 1, 0, carry)
            return past_block(j1, jnp.where(j0 + 1 < sb, 1.0, 0.0), 0, 1, mid)

        def past_run(j0, n_pairs, cr):
            for k in range(n_pairs):
                cr = past_pair(j0 + 2 * k, cr)
            return cr

        carry = (m_init, cm_first)
        done = 0
        for blocks in ATTN_TRIP_BLOCKS:
            n_trips = (sb - done) // blocks
            carry = lax.fori_loop(
                0, n_trips,
                lambda i, cr, done=done, blocks=blocks: past_run(done + blocks * i, blocks // 2, cr),
                carry)
            done = done + blocks * n_trips
        m_past, _ = lax.fori_loop(0, (sb - done + 1) // 2,
                                  lambda i, cr: past_pair(done + 2 * i, cr), carry)

        accs = []
        for h in heads:
            s = jnp.where(causal, bufs[1][:, hcols[h]], NEG_INF)
            m_new = jnp.maximum(m_past[h], jnp.max(s, axis=0, keepdims=True))
            p = jnp.exp2(s - m_new).astype(BF16)
            accs.append(acc_ref[h] * jnp.exp2(m_past[h] - m_new) + _dot(v_lhs(sb, h), p))
        cm_next = scores_into(bufs[1], 0, w_ref[jnp.minimum(sb + 1, nsb - 1)])
        ot = jnp.concatenate([a[:hd, :] / a[hd:hd + 1, :] for a in accs], axis=0)
        o_ref[pl.ds(pl.multiple_of(sb * tq, tq), tq), :] = ot.T
        return cm_next

    lax.fori_loop(0, nsb, super_block, scores_into(bufs[1], 0, w_ref[0]))


def _prompt_attn(qt, kn, vtb, xm, w_in):
    b, nq, _, qb = qt.shape
    s = kn.shape[1]
    nb, d = xm.shape[1:]
    pw = 2 * HEAD_DIM
    tq = MOBA_BLOCK
    k_col0 = ATT_WIDTH // pw
    return pl.pallas_call(
        _prompt_attn_kernel,
        grid=(b, HEAD_PAIRS),
        in_specs=[
            pl.BlockSpec((None, nq, pw, qb), lambda bi, pi: (bi, 0, pi, 0)),
            pl.BlockSpec((None, s, pw), lambda bi, pi: (bi, 0, pi)),
            pl.BlockSpec((None, nb, pw, tq), lambda bi, pi: (bi, 0, pi, 0)),
            pl.BlockSpec((None, nb, d), lambda bi, pi: (bi, 0, 0)),
            pl.BlockSpec((d, pw), lambda bi, pi: (0, k_col0 + pi)),
        ],
        out_specs=pl.BlockSpec((None, s, pw), lambda bi, pi: (bi, 0, pi)),
        out_shape=jax.ShapeDtypeStruct((b, s, ATT_WIDTH), F32),
        scratch_shapes=[
            pltpu.VMEM((nb, pw, 2 * tq), BF16),
            pltpu.VMEM((nb, 2, nb, tq), F32),
            pltpu.VMEM((tq, 2 * tq), F32), pltpu.VMEM((tq, 2 * tq), F32),
            pltpu.VMEM((2, HEAD_DIM + 2 * V7X_SUBLANES, tq), F32),
        ],
        compiler_params=pltpu.CompilerParams(
            dimension_semantics=("arbitrary", "arbitrary"),
            vmem_limit_bytes=VMEM_LIMIT_BYTES),
        name="prompt_attn",
    )(qt, kn, vtb, xm, w_in)


def _mix_out_ab(att, sg, ml, wo_ref):
    ma = (sg.astype(F32) * att).astype(BF16)
    return _dot(ma, wo_ref[:ATT_WIDTH, :]) + _dot(ml.astype(BF16), wo_ref[ATT_WIDTH:, :])


def _gmlp_in(y0, npre_ref, wi_ref, lg_ref, lb_ref):
    xn = _rms_norm(y0, npre_ref[...])
    pr = _dot(xn.astype(BF16), wi_ref[...])
    u = _gelu_tanh(pr[:, :GMLP_WIDTH])
    v = _layer_norm(_gelu_tanh(pr[:, GMLP_WIDTH:2 * GMLP_WIDTH]), lg_ref[...], lb_ref[...])
    g = pr[:, 2 * GMLP_WIDTH:]
    return u, v, g


def _prompt_out_kernel(x_ref, att_ref, sg_ref, ml_ref, np0_ref, wo_ref, npre_ref, wi_ref,
                       lg_ref, lb_ref, ws_ref, bst_ref, wc_ref, np1_ref, y_ref, mix_ref):
    tm = x_ref.shape[0]
    op = _mix_out_ab(att_ref[...], sg_ref[...], ml_ref[...], wo_ref)
    y0 = x_ref[...] + _rms_norm(op, np0_ref[...])
    u, v, g = _gmlp_in(y0, npre_ref, wi_ref, lg_ref, lb_ref)
    vb = v.astype(BF16)
    t_out = lax.broadcasted_iota(jnp.int32, (CHUNK, CHUNK), 0)
    t_in = lax.broadcasted_iota(jnp.int32, (CHUNK, CHUNK), 1)
    for gi in range(GMLP_GROUPS):
        wm = jnp.where(t_in <= t_out, ws_ref[gi], 0.0).astype(BF16)
        cols = slice(gi * GMLP_GROUP_DIM, (gi + 1) * GMLP_GROUP_DIM)
        for c in range(tm // CHUNK):
            rows = slice(c * CHUNK, (c + 1) * CHUNK)
            mix_ref[rows, cols] = _dot(wm, vb[rows, cols]) + bst_ref[:, cols]
    z = _silu(g) * (u * mix_ref[...])
    op1 = _dot(z.astype(BF16), wc_ref[...])
    y_ref[...] = y0 + _rms_norm(op1, np1_ref[...])


def _prompt_out(x2, att2, sg2, ml2, np0, wo, npre1, wi, lg, lb, ws, bst, wc, np1, tm):
    n, d = x2.shape
    const = lambda shape: pl.BlockSpec(shape, lambda i: (0,) * len(shape))
    rows = lambda width: pl.BlockSpec((tm, width), lambda i: (i, 0))
    return pl.pallas_call(
        _prompt_out_kernel,
        grid=(n // tm,),
        in_specs=[rows(d), rows(ATT_WIDTH), rows(ATT_WIDTH), rows(LRU_WIDTH),
                  const(np0.shape), const(wo.shape), const(npre1.shape), const(wi.shape),
                  const(lg.shape), const(lb.shape), const(ws.shape), const(bst.shape),
                  const(wc.shape), const(np1.shape)],
        out_specs=rows(d),
        out_shape=jax.ShapeDtypeStruct((n, d), F32),
        scratch_shapes=[pltpu.VMEM((tm, GMLP_WIDTH), F32)],
        compiler_params=pltpu.CompilerParams(
            dimension_semantics=("arbitrary",),
            vmem_limit_bytes=VMEM_LIMIT_BYTES),
        name="prompt_out",
    )(x2, att2, sg2, ml2, np0, wo, npre1, wi, lg, lb, ws, bst, wc, np1)


def _sample_in_kernel(x_ref, g_ref, whi_ref, wlo_ref, cw_ref, cb_ref, wg_ref,
                      ba_ref, bx_ref, lam_ref, h0_ref, st_ref,
                      q_ref, k_ref, v_ref, sg_ref, ml_ref, hs_ref, xl_ref):
    n = x_ref.shape[0]
    t_len = V7X_SUBLANES
    w = LRU_WIDTH
    xn = _rms_norm(x_ref[...], g_ref[...])
    q, k, rest = _in_proj_ab(xn, whi_ref, wlo_ref)
    q_ref[...] = q
    k_ref[...] = k
    v_ref[...] = rest[:, :ATT_WIDTH]
    g_att = rest[:, ATT_WIDTH:2 * ATT_WIDTH]
    x_lru = rest[:, 2 * ATT_WIDTH:2 * ATT_WIDTH + w]
    g_lru = rest[:, 2 * ATT_WIDTH + w:]
    sg_ref[...] = _silu(g_att)
    xl_ref[...] = x_lru

    tok = lax.broadcasted_iota(jnp.int32, (n, w), 0) % t_len
    st = st_ref[...]
    xc = cw_ref[CONV_WIDTH - 1:CONV_WIDTH, :] * x_lru + cb_ref[...]
    for back in range(1, CONV_WIDTH):
        prev = jnp.where(tok >= back, pltpu.roll(x_lru, back, 0),
                         pltpu.roll(st, n - t_len + back, 0))
        xc = xc + cw_ref[CONV_WIDTH - 1 - back:CONV_WIDTH - back, :] * prev

    a, u = _lru_coeffs(xc, wg_ref, ba_ref, bx_ref, lam_ref)
    for d in (1, 2, 4):
        keep = tok >= d
        u = jnp.where(keep, a * pltpu.roll(u, d, 0) + u, u)
        a = jnp.where(keep, a * pltpu.roll(a, d, 0), a)
    hs = a * h0_ref[...] + u
    hs_ref[...] = hs
    ml_ref[...] = _silu(g_lru) * hs


def _sample_in(xs, g_pre, whi, wlo, cw, cb, wg, ba, bx, lam, h0rep, stpad):
    n, d = xs.shape
    w = LRU_WIDTH
    args = (xs, g_pre, whi, wlo, cw, cb, wg, ba, bx, lam, h0rep, stpad)
    full = lambda a: pl.BlockSpec(a.shape, lambda i: (0,) * a.ndim)
    outs = [jax.ShapeDtypeStruct((n, ATT_WIDTH), F32)] * 4 + [jax.ShapeDtypeStruct((n, w), F32)] * 3
    return pl.pallas_call(
        _sample_in_kernel,
        grid=(1,),
        in_specs=[full(a) for a in args],
        out_specs=tuple(pl.BlockSpec(o.shape, lambda i: (0, 0)) for o in outs),
        out_shape=tuple(outs),
        compiler_params=pltpu.CompilerParams(
            dimension_semantics=("arbitrary",), vmem_limit_bytes=VMEM_LIMIT_BYTES),
        name="sample_in",
    )(*args)


def _sample_select_kernel(pt_ref, q_ref, ptv_ref, *refs, pages_per_step, n_blocks):
    k_refs = refs[:pages_per_step]
    idx_ref = refs[pages_per_step]
    kmt = refs[pages_per_step + 1]
    c = pl.program_id(1)
    bp = MOBA_BLOCK // PAGE_SIZE
    lane3 = lax.broadcasted_iota(jnp.int32, kmt.shape, 2)

    @pl.when(c == 0)
    def _():
        kmt[...] = jnp.zeros(kmt.shape, F32)

    for i in range(pages_per_step // bp):
        tot = k_refs[bp * i][...]
        for pg in range(1, bp):
            tot = tot + k_refs[bp * i + pg][...]
        col = jnp.sum(tot, axis=-1, keepdims=True) * (1.0 / MOBA_BLOCK)
        blk = c * (pages_per_step // bp) + i
        kmt[...] = jnp.where(lane3 == blk, col, kmt[...])

    @pl.when(c == pl.num_programs(1) - 1)
    def _():
        qv = q_ref[...]
        t_len = qv.shape[0]
        lane = lax.broadcasted_iota(jnp.int32, (t_len, V7X_LANES), 1)
        lane_pair = lax.broadcasted_iota(jnp.int32, (t_len, 2 * HEAD_DIM), 1)
        pages = jnp.broadcast_to(ptv_ref[...].astype(F32), (t_len, V7X_LANES))
        for p in range(HEAD_PAIRS):
            qp = qv[:, p * 2 * HEAD_DIM:(p + 1) * 2 * HEAD_DIM]
            kmp = kmt[2 * p:2 * p + 2].reshape(2 * HEAD_DIM, V7X_LANES)
            for hh in range(2):
                qm = jnp.where((lane_pair < HEAD_DIM) == (hh == 0), qp, 0.0)
                gate = _dot_exact(qm, kmp)
                gate = jnp.where(lane < n_blocks, gate, NEG_INF)
                out = jnp.zeros((t_len, V7X_LANES), jnp.int32)
                for r in range(MOBA_TOPK):
                    mx = jnp.max(gate, axis=-1, keepdims=True)
                    ix = jnp.min(jnp.where(gate == mx, lane, V7X_LANES), axis=-1, keepdims=True)
                    gate = jnp.where(lane == ix, NEG_INF, gate)
                    for pg in range(bp):
                        phys = jnp.sum(jnp.where(lane == ix * bp + pg, pages, 0.0),
                                       axis=-1, keepdims=True)
                        out = jnp.where(lane == r * bp + pg, phys.astype(jnp.int32), out)
                idx_ref[2 * p + hh] = out


def _sample_select(page_table, q_s, cache_t, layer, pages_per_step):
    db, n_pages = page_table.shape
    t_len = q_s.shape[0] // db
    n_blocks = n_pages * PAGE_SIZE // MOBA_BLOCK
    assert n_pages == V7X_LANES and n_pages % pages_per_step == 0

    def page_spec(i):
        return pl.BlockSpec(
            (None, None, ATT_HEADS, HEAD_DIM, PAGE_SIZE),
            lambda b, c, pt: (layer, pt[b * n_pages + c * pages_per_step + i], 0, 0, 0))

    grid_spec = pltpu.PrefetchScalarGridSpec(
        num_scalar_prefetch=1,
        grid=(db, n_pages // pages_per_step),
        in_specs=[pl.BlockSpec((t_len, ATT_WIDTH), lambda b, c, pt: (b, 0)),
                  pl.BlockSpec((None, 1, n_pages), lambda b, c, pt: (b, 0, 0))]
                 + [page_spec(i) for i in range(pages_per_step)],
        out_specs=pl.BlockSpec((None, ATT_HEADS, t_len, V7X_LANES), lambda b, c, pt: (b, 0, 0, 0)),
        scratch_shapes=[pltpu.VMEM((ATT_HEADS, HEAD_DIM, V7X_LANES), F32)],
    )
    return pl.pallas_call(
        functools.partial(_sample_select_kernel, pages_per_step=pages_per_step, n_blocks=n_blocks),
        grid_spec=grid_spec,
        out_shape=jax.ShapeDtypeStruct((db, ATT_HEADS, t_len, V7X_LANES), jnp.int32),
        compiler_params=pltpu.CompilerParams(
            dimension_semantics=("arbitrary", "arbitrary"), vmem_limit_bytes=VMEM_LIMIT_BYTES),
        name="sample_select",
    )(page_table.reshape(-1), q_s, page_table.reshape(db, 1, n_pages),
      *([cache_t] * pages_per_step))


def _sample_attn_kernel(ph_ref, qt_ref, kt_ref, vt_ref, kc_ref, vc_ref, o_ref, kbuf, vbuf, sem,
                        *, t_len, layer, heads_per_step):
    n_sel = MOBA_TOPK * (MOBA_BLOCK // PAGE_SIZE)
    tiles_per_head = t_len * n_sel
    n_tiles = heads_per_step * tiles_per_head
    groups = pl.num_programs(1)
    step = pl.program_id(0) * groups + pl.program_id(1)
    n_steps = pl.num_programs(0) * groups
    slot = step % 2

    def tile_copies(step_i, slot_i, hh, i):
        tile = hh * tiles_per_head + i
        page = ph_ref[step_i * n_tiles + tile]
        head = (step_i % groups) * heads_per_step + hh
        return (pltpu.make_async_copy(kc_ref.at[layer, page, head], kbuf.at[slot_i, tile], sem.at[0, slot_i]),
                pltpu.make_async_copy(vc_ref.at[layer, page, head], vbuf.at[slot_i, tile], sem.at[1, slot_i]))

    def for_all_tiles(step_i, slot_i, action):
        for hh in range(heads_per_step):
            def body(i, c, hh=hh):
                for cp in tile_copies(step_i, slot_i, hh, i):
                    action(cp)
                return c
            lax.fori_loop(0, tiles_per_head, body, 0, unroll=4)

    @pl.when(step == 0)
    def _():
        for_all_tiles(0, 0, lambda cp: cp.start())

    @pl.when(step + 1 < n_steps)
    def _():
        for_all_tiles(step + 1, 1 - slot, lambda cp: cp.start())

    for_all_tiles(step, slot, lambda cp: cp.wait())

    row_sel = lax.broadcasted_iota(jnp.int32, (t_len, PAGE_SIZE), 0)
    row_own = lax.broadcasted_iota(jnp.int32, (t_len, t_len), 0)
    col_own = lax.broadcasted_iota(jnp.int32, (t_len, t_len), 1)
    for hh in range(heads_per_step):
        dims = slice(hh * HEAD_DIM, (hh + 1) * HEAD_DIM)
        tile0 = hh * tiles_per_head
        ktn = kt_ref[dims, :]
        vtn = vt_ref[dims, :]
        qtn = qt_ref[dims, :] * ATT_SCALE
        s_sel = [jnp.zeros((t_len, PAGE_SIZE), F32) for _ in range(n_sel)]
        s_own = jnp.zeros((t_len, t_len), F32)
        for t in range(t_len):
            qc = qtn[:, t:t + 1]
            for i in range(n_sel):
                s = jnp.sum(kbuf[slot, tile0 + t * n_sel + i] * qc, axis=0, keepdims=True)
                s_sel[i] = jnp.where(row_sel == t, s, s_sel[i])
            s_own = jnp.where(row_own == t, jnp.sum(ktn * qc, axis=0, keepdims=True), s_own)
        s_own = jnp.where(col_own <= row_own, s_own, NEG_INF)
        m_sel = s_sel[0]
        for i in range(1, n_sel):
            m_sel = jnp.maximum(m_sel, s_sel[i])
        m = jnp.maximum(jnp.max(m_sel, axis=-1, keepdims=True),
                        jnp.max(s_own, axis=-1, keepdims=True))
        p_sel = [jnp.exp(s - m) for s in s_sel]
        p_own = jnp.exp(s_own - m)
        p_tot = p_sel[0]
        for i in range(1, n_sel):
            p_tot = p_tot + p_sel[i]
        den = jnp.sum(p_tot, axis=-1, keepdims=True) + jnp.sum(p_own, axis=-1, keepdims=True)
        for t in range(t_len):
            acc = vbuf[slot, tile0 + t * n_sel] * p_sel[0][t:t + 1, :]
            for i in range(1, n_sel):
                acc = acc + vbuf[slot, tile0 + t * n_sel + i] * p_sel[i][t:t + 1, :]
            o = (jnp.sum(acc, axis=-1, keepdims=True)
                 + jnp.sum(vtn * p_own[t:t + 1, :], axis=-1, keepdims=True))
            o_ref[dims, t:t + 1] = o / den[t:t + 1, :]


def _sample_attn(phys, qt_s, kt_s, vt_s, cache_kt, cache_vt, layer):
    db, _, t_len = qt_s.shape
    heads_per_step = 2
    n_tiles = heads_per_step * t_len * MOBA_TOPK * (MOBA_BLOCK // PAGE_SIZE)
    new_spec = pl.BlockSpec((None, heads_per_step * HEAD_DIM, t_len), lambda b, g, ph: (b, g, 0))
    cache_spec = pl.BlockSpec(memory_space=pl.ANY)
    grid_spec = pltpu.PrefetchScalarGridSpec(
        num_scalar_prefetch=1,
        grid=(db, ATT_HEADS // heads_per_step),
        in_specs=[new_spec, new_spec, new_spec, cache_spec, cache_spec],
        out_specs=new_spec,
        scratch_shapes=[
            pltpu.VMEM((2, n_tiles, HEAD_DIM, PAGE_SIZE), F32),
            pltpu.VMEM((2, n_tiles, HEAD_DIM, PAGE_SIZE), F32),
            pltpu.SemaphoreType.DMA((2, 2)),
        ],
    )
    return pl.pallas_call(
        functools.partial(_sample_attn_kernel, t_len=t_len, layer=layer,
                          heads_per_step=heads_per_step),
        grid_spec=grid_spec,
        out_shape=jax.ShapeDtypeStruct((db, ATT_WIDTH, t_len), F32),
        compiler_params=pltpu.CompilerParams(
            dimension_semantics=("arbitrary", "arbitrary"), vmem_limit_bytes=VMEM_LIMIT_BYTES),
        name="sample_attn",
    )(phys.reshape(-1), qt_s, kt_s, vt_s, cache_kt, cache_vt)


def _sample_out_kernel(x_ref, att_ref, sg_ref, ml_ref, np0_ref, wo_ref, npre_ref, wi_ref,
                       lg_ref, lb_ref, cd_ref, bst_ref, wc_ref, np1_ref, y_ref, gv_ref):
    n = x_ref.shape[0]
    t_len = cd_ref.shape[1]
    reps = n // t_len
    op = _mix_out_ab(att_ref[...], sg_ref[...], ml_ref[...], wo_ref)
    y0 = x_ref[...] + _rms_norm(op, np0_ref[...])
    u, v, g = _gmlp_in(y0, npre_ref, wi_ref, lg_ref, lb_ref)
    gv_ref[...] = v
    tile = lambda tab: jnp.concatenate([tab] * reps, axis=0)
    mix = tile(bst_ref[...]) + tile(cd_ref[0]) * v
    for d in range(1, t_len):
        mix = mix + tile(cd_ref[d]) * pltpu.roll(v, d, 0)
    z = _silu(g) * (u * mix)
    op1 = _dot(z.astype(BF16), wc_ref[...])
    y_ref[...] = y0 + _rms_norm(op1, np1_ref[...])


def _sample_out(xs, att, sg, ml, np0, wo, npre1, wi, lg, lb, cd, bst8, wc, np1):
    n, d = xs.shape
    args = (xs, att, sg, ml, np0, wo, npre1, wi, lg, lb, cd, bst8, wc, np1)
    full = lambda a: pl.BlockSpec(a.shape, lambda i: (0,) * a.ndim)
    outs = (jax.ShapeDtypeStruct((n, d), F32), jax.ShapeDtypeStruct((n, GMLP_WIDTH), F32))
    return pl.pallas_call(
        _sample_out_kernel,
        grid=(1,),
        in_specs=[full(a) for a in args],
        out_specs=tuple(pl.BlockSpec(o.shape, lambda i: (0, 0)) for o in outs),
        out_shape=outs,
        compiler_params=pltpu.CompilerParams(
            dimension_semantics=("arbitrary",), vmem_limit_bytes=VMEM_LIMIT_BYTES),
        name="sample_out",
    )(*args)


def _gate_weights(wa, wx):
    h, n, _ = wa.shape
    per = V7X_LANES // n
    eye = jnp.eye(per, dtype=wa.dtype)

    def grouped(wh):
        blocks = wh.reshape(h // per, per, n, n)
        return jnp.einsum("gpij,pq->gpiqj", blocks, eye).reshape(h // per, per * n, per * n)

    return jnp.concatenate([grouped(wa), grouped(wx)], axis=2)


def kernel(x_prompt, x_sample, cache_k, cache_v, page_table, state_lru_h, state_conv, norm_pre, norm_post, w_in_ab, conv_w, conv_b, lru_wa, lru_ba, lru_wx, lru_bx, lru_lambda, w_out_ab, w_in_c, c_ln_g, c_ln_b, c_ws, c_bs, w_out_c):
    b, s, d = x_prompt.shape
    db, t_len, _ = x_sample.shape
    n_pages = page_table.shape[1]
    assert norm_pre.shape[0] == 2 and w_in_ab.shape[0] == 1 and w_in_c.shape[0] == 1
    assert s % (2 * MOBA_BLOCK) == 0 and t_len == V7X_SUBLANES
    assert (n_pages * PAGE_SIZE) % MOBA_BLOCK == 0 and t_len <= CHUNK
    assert cache_k.shape[2:] == (PAGE_SIZE, ATT_HEADS, HEAD_DIM)
    w = LRU_WIDTH
    row = lambda vec: vec.reshape(1, -1)

    whi = w_in_ab[0].astype(BF16)
    wq = w_in_ab[0][:, :ATT_WIDTH]
    wlo = (wq - wq.astype(BF16).astype(F32)).astype(BF16)
    wg = _gate_weights(lru_wa[0], lru_wx[0]).astype(BF16)
    lru_args = (conv_w[0], row(conv_b[0]), wg, row(lru_ba[0]), row(lru_bx[0]), row(lru_lambda[0]))
    wo = w_out_ab[0].astype(BF16)
    wi = w_in_c[0].astype(BF16)
    wc = w_out_c[0].astype(BF16)
    np0, np1 = row(norm_post[0]), row(norm_post[1])
    npre0, npre1 = row(norm_pre[0]), row(norm_pre[1])
    lg, lb = row(c_ln_g[0]), row(c_ln_b[0])
    bst = jnp.repeat(c_bs[0].T, GMLP_GROUP_DIM, axis=1)

    (qt, kt, vt, vtb, kn, xm, sg, ml, h_last, conv_last) = _prompt_in(
        x_prompt, npre0, whi, wlo, *lru_args, tm=2 * MOBA_BLOCK)
    att = _prompt_attn(qt, kn, vtb, xm, w_in_ab[0])
    y_prompt = _prompt_out(
        x_prompt.reshape(b * s, d), att.reshape(b * s, ATT_WIDTH), sg.reshape(b * s, ATT_WIDTH),
        ml.reshape(b * s, w), np0, wo, npre1, wi, lg, lb, c_ws[0], bst, wc, np1,
        tm=2 * MOBA_BLOCK).reshape(b, s, d)
    heads_last = lambda xt: xt.reshape(b, ATT_HEADS, HEAD_DIM, s).transpose(0, 3, 1, 2)[None]
    k_prompt, v_prompt = heads_last(kt), heads_last(vt)

    n = db * t_len
    xs = x_sample.reshape(n, d)
    h0rep = jnp.repeat(state_lru_h[0], t_len, axis=0)
    stpad = jnp.pad(state_conv[0], ((0, 0), (t_len - (CONV_WIDTH - 1), 0), (0, 0))).reshape(n, w)
    q_s, k_s, v_s, sg_s, ml_s, hs_s, xl_s = _sample_in(xs, npre0, whi, wlo, *lru_args, h0rep, stpad)

    cache_kt = cache_k.transpose(0, 1, 3, 4, 2)
    cache_vt = cache_v.transpose(0, 1, 3, 4, 2)
    phys = _sample_select(page_table, q_s, cache_kt, 0, pages_per_step=64)
    tok_last = lambda a: a.reshape(db, t_len, ATT_WIDTH).transpose(0, 2, 1)
    att_t = _sample_attn(phys[..., :MOBA_TOPK * (MOBA_BLOCK // PAGE_SIZE)], tok_last(q_s),
                         tok_last(k_s), tok_last(v_s), cache_kt, cache_vt, 0)
    att_s = att_t.transpose(0, 2, 1).reshape(n, ATT_WIDTH)

    ws8 = c_ws[0][:, :t_len, :t_len]
    tt = jnp.arange(t_len)
    diag = lambda dd: jnp.where(tt >= dd, ws8[:, tt, jnp.maximum(tt - dd, 0)], 0.0)
    cd = jnp.stack([jnp.repeat(diag(dd).T, GMLP_GROUP_DIM, axis=1) for dd in range(t_len)])
    y_s, gv_s = _sample_out(xs, att_s, sg_s, ml_s, np0, wo, npre1, wi, lg, lb, cd, bst[:t_len],
                            wc, np1)

    per_req = lambda a, width: a.reshape(db, t_len, width)
    return (y_prompt, y_s.reshape(db, t_len, d), k_prompt, v_prompt,
            h_last.reshape(1, b, w), conv_last[None],
            k_s.reshape(1, db, t_len, ATT_HEADS, HEAD_DIM), v_s.reshape(1, db, t_len, ATT_HEADS, HEAD_DIM),
            per_req(hs_s, w)[:, t_len - 1][None], per_req(xl_s, w)[:, t_len - (CONV_WIDTH - 1):][None],
            per_req(gv_s, GMLP_WIDTH)[None])
```

```python
import functools

import jax
import jax.numpy as jnp
from jax import lax
from jax.experimental import pallas as pl
from jax.experimental.pallas import tpu as pltpu

ATT_HEADS = 8
HEAD_DIM = 64
ATT_WIDTH = ATT_HEADS * HEAD_DIM
HEAD_PAIRS = ATT_HEADS // 2
MOBA_BLOCK = 256
MOBA_TOPK = 3
QUERY_BLOCK = 128
LRU_WIDTH = 512
CONV_WIDTH = 4
LRU_C = 8.0
GMLP_WIDTH = 1024
GMLP_GROUPS = 8
GMLP_GROUP_DIM = GMLP_WIDTH // GMLP_GROUPS
CHUNK = 128
PAGE_SIZE = 128
NORM_EPS = 1e-6
ATT_SCALE = HEAD_DIM ** -0.5
LOG2_E = 1.4426950408889634

ATTN_TRIP_BLOCKS = (8, 4)
V7X_LANES = 128
V7X_SUBLANES = 8
VMEM_LIMIT_BYTES = 56 * 1024 * 1024

F32 = jnp.float32
BF16 = jnp.bfloat16
NEG_INF = float("-inf")
POS_INF = float("inf")


def _rms_norm(x, g):
    return x * lax.rsqrt(jnp.mean(x * x, axis=-1, keepdims=True) + NORM_EPS) * g


def _layer_norm(x, g, b):
    mu = jnp.mean(x, axis=-1, keepdims=True)
    xc = x - mu
    var = jnp.mean(xc * xc, axis=-1, keepdims=True)
    return xc * lax.rsqrt(var + NORM_EPS) * g + b


def _sigmoid(x):
    return 1.0 / (1.0 + jnp.exp(-x))


def _silu(x):
    return x * _sigmoid(x)


def _gelu_tanh(x):
    c = 0.7978845608028654
    return 0.5 * x * (1.0 + jnp.tanh(c * (x + 0.044715 * (x * x * x))))


def _softplus(x):
    return jnp.maximum(x, 0.0) + jnp.log1p(jnp.exp(-jnp.abs(x)))


def _dot(a, b):
    return jnp.dot(a, b, preferred_element_type=F32)


def _dot_exact(a, b):
    return jnp.dot(a, b, preferred_element_type=F32, precision=lax.Precision.HIGHEST)


def _split_bf16(x):
    hi = x.astype(BF16)
    lo = (x - hi.astype(F32)).astype(BF16)
    return hi, lo


def _in_proj_ab(xn, whi_ref, wlo_ref):
    xh, xl = _split_bf16(xn)
    q = (_dot(xh, whi_ref[:, :ATT_WIDTH]) + _dot(xh, wlo_ref[...])
         + _dot(xl, whi_ref[:, :ATT_WIDTH]))
    rest = _dot(xh, whi_ref[:, ATT_WIDTH:])
    return q, rest[:, :ATT_WIDTH], rest[:, ATT_WIDTH:]


def _lru_coeffs(xc, wg_ref, ba_ref, bx_ref, lam_ref):
    xcb = xc.astype(BF16)
    lanes = V7X_LANES
    pre = [_dot(xcb[:, g * lanes:(g + 1) * lanes], wg_ref[g]) for g in range(wg_ref.shape[0])]
    r = _sigmoid(jnp.concatenate([p[:, :lanes] for p in pre], axis=1) + ba_ref[...])
    i = _sigmoid(jnp.concatenate([p[:, lanes:] for p in pre], axis=1) + bx_ref[...])
    log_a = (-LRU_C) * r * _softplus(-lam_ref[...])
    a = jnp.exp(log_a)
    u = jnp.sqrt(-jnp.tanh(log_a) * (a * a + 1.0)) * (i * xc)
    return a, u


def _top3_rows(gate, blk, n_valid):
    nb = gate.shape[0]
    sel = jnp.zeros(gate.shape, F32)
    for r in range(MOBA_TOPK):
        mx = jnp.max(gate, axis=0, keepdims=True)
        ix = jnp.min(jnp.where(gate == mx, blk, nb), axis=0, keepdims=True)
        hit = blk == ix
        sel = jnp.maximum(sel, jnp.where(hit, jnp.where(r < n_valid, 1.0, 0.0), 0.0))
        gate = jnp.where(hit, NEG_INF, gate)
    return sel


def _prompt_in_kernel(x_ref, g_ref, whi_ref, wlo_ref, cw_ref, cb_ref, wg_ref,
                      ba_ref, bx_ref, lam_ref,
                      qt_ref, kt_ref, vt_ref, vtb_ref, kn_ref, xm_ref, sg_ref, ml_ref,
                      hl_ref, cl_ref,
                      xbuf, hcar, abuf, ubuf):
    t = pl.program_id(1)
    tm = x_ref.shape[0]
    w = LRU_WIDTH

    @pl.when(t == 0)
    def _():
        xbuf[0:V7X_SUBLANES, :] = jnp.zeros((V7X_SUBLANES, w), F32)
        hcar[...] = jnp.zeros(hcar.shape, F32)

    xn = _rms_norm(x_ref[...], g_ref[...])
    q, k, rest = _in_proj_ab(xn, whi_ref, wlo_ref)
    v = rest[:, :ATT_WIDTH]
    g_att = rest[:, ATT_WIDTH:2 * ATT_WIDTH]
    x_lru = rest[:, 2 * ATT_WIDTH:2 * ATT_WIDTH + w]
    g_lru = rest[:, 2 * ATT_WIDTH + w:]

    for s in range(tm // QUERY_BLOCK):
        rows = slice(s * QUERY_BLOCK, (s + 1) * QUERY_BLOCK)
        qt_ref[s] = q[rows, :].T
    kt_ref[...] = k.T
    vt = v.T
    vt_ref[...] = vt
    kn_ref[...] = k.astype(BF16)
    for s in range(tm // MOBA_BLOCK):
        vtb_ref[s] = vt[:, s * MOBA_BLOCK:(s + 1) * MOBA_BLOCK].astype(BF16)
        xm_ref[pl.ds(t * (tm // MOBA_BLOCK) + s, 1), :] = jnp.mean(
            xn[s * MOBA_BLOCK:(s + 1) * MOBA_BLOCK, :], axis=0, keepdims=True)
    sg_ref[...] = _silu(g_att).astype(BF16)

    xbuf[V7X_SUBLANES:, :] = x_lru
    xc = cw_ref[CONV_WIDTH - 1:CONV_WIDTH, :] * x_lru + cb_ref[...]
    for back in range(1, CONV_WIDTH):
        xc = xc + (cw_ref[CONV_WIDTH - 1 - back:CONV_WIDTH - back, :]
                   * xbuf[pl.ds(V7X_SUBLANES - back, tm), :])
    xbuf[0:V7X_SUBLANES, :] = x_lru[tm - V7X_SUBLANES:, :]

    a, u = _lru_coeffs(xc, wg_ref, ba_ref, bx_ref, lam_ref)
    abuf[...] = a
    ubuf[...] = u

    row = lax.broadcasted_iota(jnp.int32, (V7X_SUBLANES, w), 0)

    def group(gi, h):
        r0 = pl.multiple_of(gi * V7X_SUBLANES, V7X_SUBLANES)
        ag = abuf[pl.ds(r0, V7X_SUBLANES), :]
        ug = ubuf[pl.ds(r0, V7X_SUBLANES), :]
        for d in (1, 2, 4):
            keep = row >= d
            ug = jnp.where(keep, ag * pltpu.roll(ug, d, 0) + ug, ug)
            ag = jnp.where(keep, ag * pltpu.roll(ag, d, 0), ag)
        hg = ag * h + ug
        ubuf[pl.ds(r0, V7X_SUBLANES), :] = hg
        return jnp.broadcast_to(hg[V7X_SUBLANES - 1:, :], (V7X_SUBLANES, w))

    hcar[...] = lax.fori_loop(0, tm // V7X_SUBLANES, group, hcar[...], unroll=2)
    hs = ubuf[...]
    ml_ref[...] = (_silu(g_lru) * hs).astype(BF16)

    @pl.when(t == pl.num_programs(1) - 1)
    def _():
        hl_ref[...] = hs[tm - 1:, :]
        cl_ref[...] = x_lru[tm - (CONV_WIDTH - 1):, :]


def _prompt_in(x, g_pre, whi, wlo, cw, cb, wg, ba, bx, lam, tm):
    b, s, d = x.shape
    w = LRU_WIDTH
    nb = s // MOBA_BLOCK
    nq = s // QUERY_BLOCK
    const = lambda shape: pl.BlockSpec(shape, lambda bi, ti: (0,) * len(shape))
    out_shape = (
        jax.ShapeDtypeStruct((b, nq, ATT_WIDTH, QUERY_BLOCK), F32),
        jax.ShapeDtypeStruct((b, ATT_WIDTH, s), F32),
        jax.ShapeDtypeStruct((b, ATT_WIDTH, s), F32),
        jax.ShapeDtypeStruct((b, nb, ATT_WIDTH, MOBA_BLOCK), BF16),
        jax.ShapeDtypeStruct((b, s, ATT_WIDTH), BF16),
        jax.ShapeDtypeStruct((b, nb, d), F32),
        jax.ShapeDtypeStruct((b, s, ATT_WIDTH), BF16),
        jax.ShapeDtypeStruct((b, s, w), BF16),
        jax.ShapeDtypeStruct((b, 1, w), F32),
        jax.ShapeDtypeStruct((b, CONV_WIDTH - 1, w), F32),
    )
    out_specs = (
        pl.BlockSpec((None, tm // QUERY_BLOCK, ATT_WIDTH, QUERY_BLOCK), lambda bi, ti: (bi, ti, 0, 0)),
        pl.BlockSpec((None, ATT_WIDTH, tm), lambda bi, ti: (bi, 0, ti)),
        pl.BlockSpec((None, ATT_WIDTH, tm), lambda bi, ti: (bi, 0, ti)),
        pl.BlockSpec((None, tm // MOBA_BLOCK, ATT_WIDTH, MOBA_BLOCK), lambda bi, ti: (bi, ti, 0, 0)),
        pl.BlockSpec((None, tm, ATT_WIDTH), lambda bi, ti: (bi, ti, 0)),
        pl.BlockSpec((None, nb, d), lambda bi, ti: (bi, 0, 0)),
        pl.BlockSpec((None, tm, ATT_WIDTH), lambda bi, ti: (bi, ti, 0)),
        pl.BlockSpec((None, tm, w), lambda bi, ti: (bi, ti, 0)),
        pl.BlockSpec((None, 1, w), lambda bi, ti: (bi, 0, 0)),
        pl.BlockSpec((None, CONV_WIDTH - 1, w), lambda bi, ti: (bi, 0, 0)),
    )
    in_specs = [
        pl.BlockSpec((None, tm, d), lambda bi, ti: (bi, ti, 0)),
        const((1, d)), const(whi.shape), const(wlo.shape), const(cw.shape), const(cb.shape),
        const(wg.shape), const(ba.shape), const(bx.shape), const(lam.shape),
    ]
    return pl.pallas_call(
        _prompt_in_kernel,
        grid=(b, s // tm),
        in_specs=in_specs,
        out_specs=out_specs,
        out_shape=out_shape,
        scratch_shapes=[
            pltpu.VMEM((tm + V7X_SUBLANES, w), F32),
            pltpu.VMEM((V7X_SUBLANES, w), F32),
            pltpu.VMEM((tm, w), F32),
            pltpu.VMEM((tm, w), F32),
        ],
        compiler_params=pltpu.CompilerParams(
            dimension_semantics=("arbitrary", "arbitrary"),
            vmem_limit_bytes=VMEM_LIMIT_BYTES),
        name="prompt_in",
    )(x, g_pre, whi, wlo, cw, cb, wg, ba, bx, lam)


def _prompt_attn_kernel(qt_ref, kn_ref, vtb_ref, xm_ref, wk_ref, o_ref, w_ref, sel_ref, sa_ref,
                        sb_ref, acc_ref):
    nsb = vtb_ref.shape[0]
    nb = xm_ref.shape[0]
    tq = MOBA_BLOCK
    hd = HEAD_DIM
    ones_rows = 2 * V7X_SUBLANES
    first = lax.broadcasted_iota(jnp.int32, (2 * hd, tq), 0) < hd
    kmp = _dot_exact(xm_ref[...], wk_ref[...])
    blk = lax.broadcasted_iota(jnp.int32, (nb, tq), 0)
    ones_tile = jnp.ones((ones_rows, tq), BF16)
    causal = (lax.broadcasted_iota(jnp.int32, (tq, tq), 0)
              <= lax.broadcasted_iota(jnp.int32, (tq, tq), 1))

    def stage(sb, c):
        qt = jnp.concatenate([qt_ref[2 * sb], qt_ref[2 * sb + 1]], axis=1)
        for h in range(2):
            qth = jnp.where(first, qt, 0.0) if h == 0 else jnp.where(first, 0.0, qt)
            gate = _dot_exact(kmp, qth)
            gate = jnp.where(blk < sb, gate, NEG_INF)
            sel_ref[sb, h] = _top3_rows(gate, blk, sb)
            w_ref[sb, :, h * tq:(h + 1) * tq] = (qth * (ATT_SCALE * LOG2_E)).astype(BF16)
        return c

    lax.fori_loop(0, nsb, stage, 0, unroll=2)

    def v_lhs(j, h):
        return jnp.concatenate([vtb_ref[j, h * hd:(h + 1) * hd, :], ones_tile], axis=0)

    def key_block(j):
        return kn_ref[pl.ds(pl.multiple_of(j * MOBA_BLOCK, MOBA_BLOCK), MOBA_BLOCK), :]

    heads = (0, 1)
    bufs = (sa_ref, sb_ref)
    hcols = tuple(slice(h * tq, (h + 1) * tq) for h in heads)

    def scores_into(buf, j, wq):
        sc = _dot(key_block(j), wq)
        buf[...] = sc
        return tuple(jnp.max(sc[:, hcols[h]], axis=0, keepdims=True) for h in heads)

    def super_block(sb, cm_first):
        last = sb - 1
        wq = w_ref[sb]
        for h in heads:
            acc_ref[h] = jnp.zeros(acc_ref.shape[1:], F32)
        m_init = tuple(jnp.full((1, tq), NEG_INF, F32) for _ in heads)

        def past_block(j, live, cur, nxt, carry):
            m_old, cm = carry
            cm_next = scores_into(bufs[nxt], jnp.minimum(j + 1, sb), wq)
            m_out = []
            for h in heads:
                on = sel_ref[sb, h, pl.ds(j, 1), :] * live > 0.0
                m_new = jnp.maximum(m_old[h], jnp.where(on, cm[h], NEG_INF))
                p = jnp.exp2(bufs[cur][:, hcols[h]] - jnp.where(on, m_new, POS_INF)).astype(BF16)
                alpha = jnp.exp2(m_old[h] - jnp.where(m_new == NEG_INF, 0.0, m_new))
                acc_ref[h] = acc_ref[h] * alpha + _dot(v_lhs(j, h), p)
                m_out.append(m_new)
            return tuple(m_out), cm_next

        def past_pair(j0, carry):
            j1 = jnp.minimum(j0 + 1, last)
            mid = past_block(j0, 1.0, 1, 0, carry)
            return past_block(j1, jnp.where(j0 + 1 < sb, 1.0, 0.0), 0, 1, mid)

        def past_run(j0, n_pairs, cr):
            for k in range(n_pairs):
                cr = past_pair(j0 + 2 * k, cr)
            return cr

        carry = (m_init, cm_first)
        done = 0
        for blocks in ATTN_TRIP_BLOCKS:
            n_trips = (sb - done) // blocks
            carry = lax.fori_loop(
                0, n_trips,
                lambda i, cr, done=done, blocks=blocks: past_run(done + blocks * i, blocks // 2, cr),
                carry)
            done = done + blocks * n_trips
        m_past, _ = lax.fori_loop(0, (sb - done + 1) // 2,
                                  lambda i, cr: past_pair(done + 2 * i, cr), carry)

        accs = []
        for h in heads:
            s = jnp.where(causal, bufs[1][:, hcols[h]], NEG_INF)
            m_new = jnp.maximum(m_past[h], jnp.max(s, axis=0, keepdims=True))
            p = jnp.exp2(s - m_new).astype(BF16)
            accs.append(acc_ref[h] * jnp.exp2(m_past[h] - m_new) + _dot(v_lhs(sb, h), p))
        cm_next = scores_into(bufs[1], 0, w_ref[jnp.minimum(sb + 1, nsb - 1)])
        ot = jnp.concatenate([a[:hd, :] / a[hd:hd + 1, :] for a in accs], axis=0)
        o_ref[pl.ds(pl.multiple_of(sb * tq, tq), tq), :] = ot.T
        return cm_next

    lax.fori_loop(0, nsb, super_block, scores_into(bufs[1], 0, w_ref[0]))


def _prompt_attn(qt, kn, vtb, xm, w_in):
    b, nq, _, qb = qt.shape
    s = kn.shape[1]
    nb, d = xm.shape[1:]
    pw = 2 * HEAD_DIM
    tq = MOBA_BLOCK
    k_col0 = ATT_WIDTH // pw
    return pl.pallas_call(
        _prompt_attn_kernel,
        grid=(b, HEAD_PAIRS),
        in_specs=[
            pl.BlockSpec((None, nq, pw, qb), lambda bi, pi: (bi, 0, pi, 0)),
            pl.BlockSpec((None, s, pw), lambda bi, pi: (bi, 0, pi)),
            pl.BlockSpec((None, nb, pw, tq), lambda bi, pi: (bi, 0, pi, 0)),
            pl.BlockSpec((None, nb, d), lambda bi, pi: (bi, 0, 0)),
            pl.BlockSpec((d, pw), lambda bi, pi: (0, k_col0 + pi)),
        ],
        out_specs=pl.BlockSpec((None, s, pw), lambda bi, pi: (bi, 0, pi)),
        out_shape=jax.ShapeDtypeStruct((b, s, ATT_WIDTH), F32),
        scratch_shapes=[
            pltpu.VMEM((nb, pw, 2 * tq), BF16),
            pltpu.VMEM((nb, 2, nb, tq), F32),
            pltpu.VMEM((tq, 2 * tq), F32), pltpu.VMEM((tq, 2 * tq), F32),
            pltpu.VMEM((2, HEAD_DIM + 2 * V7X_SUBLANES, tq), F32),
        ],
        compiler_params=pltpu.CompilerParams(
            dimension_semantics=("arbitrary", "arbitrary"),
            vmem_limit_bytes=VMEM_LIMIT_BYTES),
        name="prompt_attn",
    )(qt, kn, vtb, xm, w_in)


def _mix_out_ab(att, sg, ml, wo_ref):
    ma = (sg.astype(F32) * att).astype(BF16)
    return _dot(ma, wo_ref[:ATT_WIDTH, :]) + _dot(ml.astype(BF16), wo_ref[ATT_WIDTH:, :])


def _gmlp_in(y0, npre_ref, wi_ref, lg_ref, lb_ref):
    xn = _rms_norm(y0, npre_ref[...])
    pr = _dot(xn.astype(BF16), wi_ref[...])
    u = _gelu_tanh(pr[:, :GMLP_WIDTH])
    v = _layer_norm(_gelu_tanh(pr[:, GMLP_WIDTH:2 * GMLP_WIDTH]), lg_ref[...], lb_ref[...])
    g = pr[:, 2 * GMLP_WIDTH:]
    return u, v, g


def _prompt_out_kernel(x_ref, att_ref, sg_ref, ml_ref, np0_ref, wo_ref, npre_ref, wi_ref,
                       lg_ref, lb_ref, ws_ref, bst_ref, wc_ref, np1_ref, y_ref, mix_ref):
    tm = x_ref.shape[0]
    op = _mix_out_ab(att_ref[...], sg_ref[...], ml_ref[...], wo_ref)
    y0 = x_ref[...] + _rms_norm(op, np0_ref[...])
    u, v, g = _gmlp_in(y0, npre_ref, wi_ref, lg_ref, lb_ref)
    vb = v.astype(BF16)
    t_out = lax.broadcasted_iota(jnp.int32, (CHUNK, CHUNK), 0)
    t_in = lax.broadcasted_iota(jnp.int32, (CHUNK, CHUNK), 1)
    for gi in range(GMLP_GROUPS):
        wm = jnp.where(t_in <= t_out, ws_ref[gi], 0.0).astype(BF16)
        cols = slice(gi * GMLP_GROUP_DIM, (gi + 1) * GMLP_GROUP_DIM)
        for c in range(tm // CHUNK):
            rows = slice(c * CHUNK, (c + 1) * CHUNK)
            mix_ref[rows, cols] = _dot(wm, vb[rows, cols]) + bst_ref[:, cols]
    z = _silu(g) * (u * mix_ref[...])
    op1 = _dot(z.astype(BF16), wc_ref[...])
    y_ref[...] = y0 + _rms_norm(op1, np1_ref[...])


def _prompt_out(x2, att2, sg2, ml2, np0, wo, npre1, wi, lg, lb, ws, bst, wc, np1, tm):
    n, d = x2.shape
    const = lambda shape: pl.BlockSpec(shape, lambda i: (0,) * len(shape))
    rows = lambda width: pl.BlockSpec((tm, width), lambda i: (i, 0))
    return pl.pallas_call(
        _prompt_out_kernel,
        grid=(n // tm,),
        in_specs=[rows(d), rows(ATT_WIDTH), rows(ATT_WIDTH), rows(LRU_WIDTH),
                  const(np0.shape), const(wo.shape), const(npre1.shape), const(wi.shape),
                  const(lg.shape), const(lb.shape), const(ws.shape), const(bst.shape),
                  const(wc.shape), const(np1.shape)],
        out_specs=rows(d),
        out_shape=jax.ShapeDtypeStruct((n, d), F32),
        scratch_shapes=[pltpu.VMEM((tm, GMLP_WIDTH), F32)],
        compiler_params=pltpu.CompilerParams(
            dimension_semantics=("arbitrary",),
            vmem_limit_bytes=VMEM_LIMIT_BYTES),
        name="prompt_out",
    )(x2, att2, sg2, ml2, np0, wo, npre1, wi, lg, lb, ws, bst, wc, np1)


def _sample_in_kernel(x_ref, g_ref, whi_ref, wlo_ref, cw_ref, cb_ref, wg_ref,
                      ba_ref, bx_ref, lam_ref, h0_ref, st_ref,
                      q_ref, k_ref, v_ref, sg_ref, ml_ref, hs_ref, xl_ref):
    n = x_ref.shape[0]
    t_len = V7X_SUBLANES
    w = LRU_WIDTH
    xn = _rms_norm(x_ref[...], g_ref[...])
    q, k, rest = _in_proj_ab(xn, whi_ref, wlo_ref)
    q_ref[...] = q
    k_ref[...] = k
    v_ref[...] = rest[:, :ATT_WIDTH]
    g_att = rest[:, ATT_WIDTH:2 * ATT_WIDTH]
    x_lru = rest[:, 2 * ATT_WIDTH:2 * ATT_WIDTH + w]
    g_lru = rest[:, 2 * ATT_WIDTH + w:]
    sg_ref[...] = _silu(g_att)
    xl_ref[...] = x_lru

    tok = lax.broadcasted_iota(jnp.int32, (n, w), 0) % t_len
    st = st_ref[...]
    xc = cw_ref[CONV_WIDTH - 1:CONV_WIDTH, :] * x_lru + cb_ref[...]
    for back in range(1, CONV_WIDTH):
        prev = jnp.where(tok >= back, pltpu.roll(x_lru, back, 0),
                         pltpu.roll(st, n - t_len + back, 0))
        xc = xc + cw_ref[CONV_WIDTH - 1 - back:CONV_WIDTH - back, :] * prev

    a, u = _lru_coeffs(xc, wg_ref, ba_ref, bx_ref, lam_ref)
    for d in (1, 2, 4):
        keep = tok >= d
        u = jnp.where(keep, a * pltpu.roll(u, d, 0) + u, u)
        a = jnp.where(keep, a * pltpu.roll(a, d, 0), a)
    hs = a * h0_ref[...] + u
    hs_ref[...] = hs
    ml_ref[...] = _silu(g_lru) * hs


def _sample_in(xs, g_pre, whi, wlo, cw, cb, wg, ba, bx, lam, h0rep, stpad):
    n, d = xs.shape
    w = LRU_WIDTH
    args = (xs, g_pre, whi, wlo, cw, cb, wg, ba, bx, lam, h0rep, stpad)
    full = lambda a: pl.BlockSpec(a.shape, lambda i: (0,) * a.ndim)
    outs = [jax.ShapeDtypeStruct((n, ATT_WIDTH), F32)] * 4 + [jax.ShapeDtypeStruct((n, w), F32)] * 3
    return pl.pallas_call(
        _sample_in_kernel,
        grid=(1,),
        in_specs=[full(a) for a in args],
        out_specs=tuple(pl.BlockSpec(o.shape, lambda i: (0, 0)) for o in outs),
        out_shape=tuple(outs),
        compiler_params=pltpu.CompilerParams(
            dimension_semantics=("arbitrary",), vmem_limit_bytes=VMEM_LIMIT_BYTES),
        name="sample_in",
    )(*args)


def _sample_select_kernel(pt_ref, q_ref, ptv_ref, *refs, pages_per_step, n_blocks):
    k_refs = refs[:pages_per_step]
    idx_ref = refs[pages_per_step]
    kmt = refs[pages_per_step + 1]
    c = pl.program_id(1)
    bp = MOBA_BLOCK // PAGE_SIZE
    lane3 = lax.broadcasted_iota(jnp.int32, kmt.shape, 2)

    @pl.when(c == 0)
    def _():
        kmt[...] = jnp.zeros(kmt.shape, F32)

    for i in range(pages_per_step // bp):
        tot = k_refs[bp * i][...]
        for pg in range(1, bp):
            tot = tot + k_refs[bp * i + pg][...]
        col = jnp.sum(tot, axis=-1, keepdims=True) * (1.0 / MOBA_BLOCK)
        blk = c * (pages_per_step // bp) + i
        kmt[...] = jnp.where(lane3 == blk, col, kmt[...])

    @pl.when(c == pl.num_programs(1) - 1)
    def _():
        qv = q_ref[...]
        t_len = qv.shape[0]
        lane = lax.broadcasted_iota(jnp.int32, (t_len, V7X_LANES), 1)
        lane_pair = lax.broadcasted_iota(jnp.int32, (t_len, 2 * HEAD_DIM), 1)
        pages = jnp.broadcast_to(ptv_ref[...].astype(F32), (t_len, V7X_LANES))
        for p in range(HEAD_PAIRS):
            qp = qv[:, p * 2 * HEAD_DIM:(p + 1) * 2 * HEAD_DIM]
            kmp = kmt[2 * p:2 * p + 2].reshape(2 * HEAD_DIM, V7X_LANES)
            for hh in range(2):
                qm = jnp.where((lane_pair < HEAD_DIM) == (hh == 0), qp, 0.0)
                gate = _dot_exact(qm, kmp)
                gate = jnp.where(lane < n_blocks, gate, NEG_INF)
                out = jnp.zeros((t_len, V7X_LANES), jnp.int32)
                for r in range(MOBA_TOPK):
                    mx = jnp.max(gate, axis=-1, keepdims=True)
                    ix = jnp.min(jnp.where(gate == mx, lane, V7X_LANES), axis=-1, keepdims=True)
                    gate = jnp.where(lane == ix, NEG_INF, gate)
                    for pg in range(bp):
                        phys = jnp.sum(jnp.where(lane == ix * bp + pg, pages, 0.0),
                                       axis=-1, keepdims=True)
                        out = jnp.where(lane == r * bp + pg, phys.astype(jnp.int32), out)
                idx_ref[2 * p + hh] = out


def _sample_select(page_table, q_s, cache_t, layer, pages_per_step):
    db, n_pages = page_table.shape
    t_len = q_s.shape[0] // db
    n_blocks = n_pages * PAGE_SIZE // MOBA_BLOCK
    assert n_pages == V7X_LANES and n_pages % pages_per_step == 0

    def page_spec(i):
        return pl.BlockSpec(
            (None, None, ATT_HEADS, HEAD_DIM, PAGE_SIZE),
            lambda b, c, pt: (layer, pt[b * n_pages + c * pages_per_step + i], 0, 0, 0))

    grid_spec = pltpu.PrefetchScalarGridSpec(
        num_scalar_prefetch=1,
        grid=(db, n_pages // pages_per_step),
        in_specs=[pl.BlockSpec((t_len, ATT_WIDTH), lambda b, c, pt: (b, 0)),
                  pl.BlockSpec((None, 1, n_pages), lambda b, c, pt: (b, 0, 0))]
                 + [page_spec(i) for i in range(pages_per_step)],
        out_specs=pl.BlockSpec((None, ATT_HEADS, t_len, V7X_LANES), lambda b, c, pt: (b, 0, 0, 0)),
        scratch_shapes=[pltpu.VMEM((ATT_HEADS, HEAD_DIM, V7X_LANES), F32)],
    )
    return pl.pallas_call(
        functools.partial(_sample_select_kernel, pages_per_step=pages_per_step, n_blocks=n_blocks),
        grid_spec=grid_spec,
        out_shape=jax.ShapeDtypeStruct((db, ATT_HEADS, t_len, V7X_LANES), jnp.int32),
        compiler_params=pltpu.CompilerParams(
            dimension_semantics=("arbitrary", "arbitrary"), vmem_limit_bytes=VMEM_LIMIT_BYTES),
        name="sample_select",
    )(page_table.reshape(-1), q_s, page_table.reshape(db, 1, n_pages),
      *([cache_t] * pages_per_step))


def _sample_attn_kernel(ph_ref, qt_ref, kt_ref, vt_ref, kc_ref, vc_ref, o_ref, kbuf, vbuf, sem,
                        *, t_len, layer, heads_per_step):
    n_sel = MOBA_TOPK * (MOBA_BLOCK // PAGE_SIZE)
    tiles_per_head = t_len * n_sel
    n_tiles = heads_per_step * tiles_per_head
    groups = pl.num_programs(1)
    step = pl.program_id(0) * groups + pl.program_id(1)
    n_steps = pl.num_programs(0) * groups
    slot = step % 2

    def tile_copies(step_i, slot_i, hh, i):
        tile = hh * tiles_per_head + i
        page = ph_ref[step_i * n_tiles + tile]
        head = (step_i % groups) * heads_per_step + hh
        return (pltpu.make_async_copy(kc_ref.at[layer, page, head], kbuf.at[slot_i, tile], sem.at[0, slot_i]),
                pltpu.make_async_copy(vc_ref.at[layer, page, head], vbuf.at[slot_i, tile], sem.at[1, slot_i]))

    def for_all_tiles(step_i, slot_i, action):
        for hh in range(heads_per_step):
            def body(i, c, hh=hh):
                for cp in tile_copies(step_i, slot_i, hh, i):
                    action(cp)
                return c
            lax.fori_loop(0, tiles_per_head, body, 0, unroll=4)

    @pl.when(step == 0)
    def _():
        for_all_tiles(0, 0, lambda cp: cp.start())

    @pl.when(step + 1 < n_steps)
    def _():
        for_all_tiles(step + 1, 1 - slot, lambda cp: cp.start())

    for_all_tiles(step, slot, lambda cp: cp.wait())

    row_sel = lax.broadcasted_iota(jnp.int32, (t_len, PAGE_SIZE), 0)
    row_own = lax.broadcasted_iota(jnp.int32, (t_len, t_len), 0)
    col_own = lax.broadcasted_iota(jnp.int32, (t_len, t_len), 1)
    for hh in range(heads_per_step):
        dims = slice(hh * HEAD_DIM, (hh + 1) * HEAD_DIM)
        tile0 = hh * tiles_per_head
        ktn = kt_ref[dims, :]
        vtn = vt_ref[dims, :]
        qtn = qt_ref[dims, :] * ATT_SCALE
        s_sel = [jnp.zeros((t_len, PAGE_SIZE), F32) for _ in range(n_sel)]
        s_own = jnp.zeros((t_len, t_len), F32)
        for t in range(t_len):
            qc = qtn[:, t:t + 1]
            for i in range(n_sel):
                s = jnp.sum(kbuf[slot, tile0 + t * n_sel + i] * qc, axis=0, keepdims=True)
                s_sel[i] = jnp.where(row_sel == t, s, s_sel[i])
            s_own = jnp.where(row_own == t, jnp.sum(ktn * qc, axis=0, keepdims=True), s_own)
        s_own = jnp.where(col_own <= row_own, s_own, NEG_INF)
        m_sel = s_sel[0]
        for i in range(1, n_sel):
            m_sel = jnp.maximum(m_sel, s_sel[i])
        m = jnp.maximum(jnp.max(m_sel, axis=-1, keepdims=True),
                        jnp.max(s_own, axis=-1, keepdims=True))
        p_sel = [jnp.exp(s - m) for s in s_sel]
        p_own = jnp.exp(s_own - m)
        p_tot = p_sel[0]
        for i in range(1, n_sel):
            p_tot = p_tot + p_sel[i]
        den = jnp.sum(p_tot, axis=-1, keepdims=True) + jnp.sum(p_own, axis=-1, keepdims=True)
        for t in range(t_len):
            acc = vbuf[slot, tile0 + t * n_sel] * p_sel[0][t:t + 1, :]
            for i in range(1, n_sel):
                acc = acc + vbuf[slot, tile0 + t * n_sel + i] * p_sel[i][t:t + 1, :]
            o = (jnp.sum(acc, axis=-1, keepdims=True)
                 + jnp.sum(vtn * p_own[t:t + 1, :], axis=-1, keepdims=True))
            o_ref[dims, t:t + 1] = o / den[t:t + 1, :]


def _sample_attn(phys, qt_s, kt_s, vt_s, cache_kt, cache_vt, layer):
    db, _, t_len = qt_s.shape
    heads_per_step = 4
    n_tiles = heads_per_step * t_len * MOBA_TOPK * (MOBA_BLOCK // PAGE_SIZE)
    new_spec = pl.BlockSpec((None, heads_per_step * HEAD_DIM, t_len), lambda b, g, ph: (b, g, 0))
    cache_spec = pl.BlockSpec(memory_space=pl.ANY)
    grid_spec = pltpu.PrefetchScalarGridSpec(
        num_scalar_prefetch=1,
        grid=(db, ATT_HEADS // heads_per_step),
        in_specs=[new_spec, new_spec, new_spec, cache_spec, cache_spec],
        out_specs=new_spec,
        scratch_shapes=[
            pltpu.VMEM((2, n_tiles, HEAD_DIM, PAGE_SIZE), F32),
            pltpu.VMEM((2, n_tiles, HEAD_DIM, PAGE_SIZE), F32),
            pltpu.SemaphoreType.DMA((2, 2)),
        ],
    )
    return pl.pallas_call(
        functools.partial(_sample_attn_kernel, t_len=t_len, layer=layer,
                          heads_per_step=heads_per_step),
        grid_spec=grid_spec,
        out_shape=jax.ShapeDtypeStruct((db, ATT_WIDTH, t_len), F32),
        compiler_params=pltpu.CompilerParams(
            dimension_semantics=("arbitrary", "arbitrary"), vmem_limit_bytes=VMEM_LIMIT_BYTES),
        name="sample_attn",
    )(phys.reshape(-1), qt_s, kt_s, vt_s, cache_kt, cache_vt)


def _sample_out_kernel(x_ref, att_ref, sg_ref, ml_ref, np0_ref, wo_ref, npre_ref, wi_ref,
                       lg_ref, lb_ref, cd_ref, bst_ref, wc_ref, np1_ref, y_ref, gv_ref):
    n = x_ref.shape[0]
    t_len = cd_ref.shape[1]
    reps = n // t_len
    op = _mix_out_ab(att_ref[...], sg_ref[...], ml_ref[...], wo_ref)
    y0 = x_ref[...] + _rms_norm(op, np0_ref[...])
    u, v, g = _gmlp_in(y0, npre_ref, wi_ref, lg_ref, lb_ref)
    gv_ref[...] = v
    tile = lambda tab: jnp.concatenate([tab] * reps, axis=0)
    mix = tile(bst_ref[...]) + tile(cd_ref[0]) * v
    for d in range(1, t_len):
        mix = mix + tile(cd_ref[d]) * pltpu.roll(v, d, 0)
    z = _silu(g) * (u * mix)
    op1 = _dot(z.astype(BF16), wc_ref[...])
    y_ref[...] = y0 + _rms_norm(op1, np1_ref[...])


def _sample_out(xs, att, sg, ml, np0, wo, npre1, wi, lg, lb, cd, bst8, wc, np1):
    n, d = xs.shape
    args = (xs, att, sg, ml, np0, wo, npre1, wi, lg, lb, cd, bst8, wc, np1)
    full = lambda a: pl.BlockSpec(a.shape, lambda i: (0,) * a.ndim)
    outs = (jax.ShapeDtypeStruct((n, d), F32), jax.ShapeDtypeStruct((n, GMLP_WIDTH), F32))
    return pl.pallas_call(
        _sample_out_kernel,
        grid=(1,),
        in_specs=[full(a) for a in args],
        out_specs=tuple(pl.BlockSpec(o.shape, lambda i: (0, 0)) for o in outs),
        out_shape=outs,
        compiler_params=pltpu.CompilerParams(
            dimension_semantics=("arbitrary",), vmem_limit_bytes=VMEM_LIMIT_BYTES),
        name="sample_out",
    )(*args)


def _gate_weights(wa, wx):
    h, n, _ = wa.shape
    per = V7X_LANES // n
    eye = jnp.eye(per, dtype=wa.dtype)

    def grouped(wh):
        blocks = wh.reshape(h // per, per, n, n)
        return jnp.einsum("gpij,pq->gpiqj", blocks, eye).reshape(h // per, per * n, per * n)

    return jnp.concatenate([grouped(wa), grouped(wx)], axis=2)


def kernel(x_prompt, x_sample, cache_k, cache_v, page_table, state_lru_h, state_conv, norm_pre, norm_post, w_in_ab, conv_w, conv_b, lru_wa, lru_ba, lru_wx, lru_bx, lru_lambda, w_out_ab, w_in_c, c_ln_g, c_ln_b, c_ws, c_bs, w_out_c):
    b, s, d = x_prompt.shape
    db, t_len, _ = x_sample.shape
    n_pages = page_table.shape[1]
    assert norm_pre.shape[0] == 2 and w_in_ab.shape[0] == 1 and w_in_c.shape[0] == 1
    assert s % (2 * MOBA_BLOCK) == 0 and t_len == V7X_SUBLANES
    assert (n_pages * PAGE_SIZE) % MOBA_BLOCK == 0 and t_len <= CHUNK
    assert cache_k.shape[2:] == (PAGE_SIZE, ATT_HEADS, HEAD_DIM)
    w = LRU_WIDTH
    row = lambda vec: vec.reshape(1, -1)

    whi = w_in_ab[0].astype(BF16)
    wq = w_in_ab[0][:, :ATT_WIDTH]
    wlo = (wq - wq.astype(BF16).astype(F32)).astype(BF16)
    wg = _gate_weights(lru_wa[0], lru_wx[0]).astype(BF16)
    lru_args = (conv_w[0], row(conv_b[0]), wg, row(lru_ba[0]), row(lru_bx[0]), row(lru_lambda[0]))
    wo = w_out_ab[0].astype(BF16)
    wi = w_in_c[0].astype(BF16)
    wc = w_out_c[0].astype(BF16)
    np0, np1 = row(norm_post[0]), row(norm_post[1])
    npre0, npre1 = row(norm_pre[0]), row(norm_pre[1])
    lg, lb = row(c_ln_g[0]), row(c_ln_b[0])
    bst = jnp.repeat(c_bs[0].T, GMLP_GROUP_DIM, axis=1)

    (qt, kt, vt, vtb, kn, xm, sg, ml, h_last, conv_last) = _prompt_in(
        x_prompt, npre0, whi, wlo, *lru_args, tm=2 * MOBA_BLOCK)
    att = _prompt_attn(qt, kn, vtb, xm, w_in_ab[0])
    y_prompt = _prompt_out(
        x_prompt.reshape(b * s, d), att.reshape(b * s, ATT_WIDTH), sg.reshape(b * s, ATT_WIDTH),
        ml.reshape(b * s, w), np0, wo, npre1, wi, lg, lb, c_ws[0], bst, wc, np1,
        tm=2 * MOBA_BLOCK).reshape(b, s, d)
    heads_last = lambda xt: xt.reshape(b, ATT_HEADS, HEAD_DIM, s).transpose(0, 3, 1, 2)[None]
    k_prompt, v_prompt = heads_last(kt), heads_last(vt)

    n = db * t_len
    xs = x_sample.reshape(n, d)
    h0rep = jnp.repeat(state_lru_h[0], t_len, axis=0)
    stpad = jnp.pad(state_conv[0], ((0, 0), (t_len - (CONV_WIDTH - 1), 0), (0, 0))).reshape(n, w)
    q_s, k_s, v_s, sg_s, ml_s, hs_s, xl_s = _sample_in(xs, npre0, whi, wlo, *lru_args, h0rep, stpad)

    cache_kt = cache_k.transpose(0, 1, 3, 4, 2)
    cache_vt = cache_v.transpose(0, 1, 3, 4, 2)
    phys = _sample_select(page_table, q_s, cache_kt, 0, pages_per_step=64)
    tok_last = lambda a: a.reshape(db, t_len, ATT_WIDTH).transpose(0, 2, 1)
    att_t = _sample_attn(phys[..., :MOBA_TOPK * (MOBA_BLOCK // PAGE_SIZE)], tok_last(q_s),
                         tok_last(k_s), tok_last(v_s), cache_kt, cache_vt, 0)
    att_s = att_t.transpose(0, 2, 1).reshape(n, ATT_WIDTH)

    ws8 = c_ws[0][:, :t_len, :t_len]
    tt = jnp.arange(t_len)
    diag = lambda dd: jnp.where(tt >= dd, ws8[:, tt, jnp.maximum(tt - dd, 0)], 0.0)
    cd = jnp.stack([jnp.repeat(diag(dd).T, GMLP_GROUP_DIM, axis=1) for dd in range(t_len)])
    y_s, gv_s = _sample_out(xs, att_s, sg_s, ml_s, np0, wo, npre1, wi, lg, lb, cd, bst[:t_len],
                            wc, np1)

    per_req = lambda a, width: a.reshape(db, t_len, width)
    return (y_prompt, y_s.reshape(db, t_len, d), k_prompt, v_prompt,
            h_last.reshape(1, b, w), conv_last[None],
            k_s.reshape(1, db, t_len, ATT_HEADS, HEAD_DIM), v_s.reshape(1, db, t_len, ATT_HEADS, HEAD_DIM),
            per_req(hs_s, w)[:, t_len - 1][None], per_req(xl_s, w)[:, t_len - (CONV_WIDTH - 1):][None],
            per_req(gv_s, GMLP_WIDTH)[None])
```
